```python
import jax, jax.numpy as jnp
from jax import lax
import numpy as np

D_MODEL = 1024
BATCH = 16
SEQ = 4096
DEPTH = 1
DEC_BATCH = 16
DEC_SEQ = 16
PAST_LEN = 1024

CHUNK = 64
D_CONV = D_MODEL
CONV_W = 3
RET_HEADS = 4
RET_DK = D_MODEL // RET_HEADS
RET_DV = D_MODEL // RET_HEADS
D_RET_V = RET_HEADS * RET_DV
D_MIX = D_MODEL
ROPE_BASE = 10000.0
PEER_HEADS = 8
N_KEYS = 128
N_EXPERTS = N_KEYS * N_KEYS
PEER_TOPK = 16
D_KEY = 256
D_HALF = D_KEY // 2
PEER_BLOCK = 128
EPS = 1e-6
IN_SPLITS = (D_CONV, D_CONV, D_CONV, RET_HEADS * RET_DK, RET_HEADS * RET_DK, D_RET_V, D_RET_V, D_MIX, D_MIX)
D_IN = 3 * D_CONV + 2 * RET_HEADS * RET_DK + 2 * D_RET_V + 2 * D_MIX

kernel_name = 'hybrid_conv_retention_peer_stream_step'


def _rmsnorm(x, g):
    xf = x.astype(jnp.float32)
    y = xf * lax.rsqrt(jnp.mean(xf * xf, axis=-1, keepdims=True) + EPS)
    return (y * g.astype(jnp.float32)).astype(x.dtype)


def _ret_log_decay():
    return jnp.log1p(-jnp.exp2(-5.0 - jnp.arange(RET_HEADS, dtype=jnp.float32)))


def _rope(x, pos):
    half = x.shape[-1] // 2
    inv = ROPE_BASE ** (-jnp.arange(half, dtype=jnp.float32) / half)
    ang = pos.astype(jnp.float32)[:, None] * inv[None, :]
    cos = jnp.cos(ang)[None, :, None, :]
    sin = jnp.sin(ang)[None, :, None, :]
    x1 = x[..., :half].astype(jnp.float32)
    x2 = x[..., half:].astype(jnp.float32)
    return jnp.concatenate([x1 * cos - x2 * sin, x1 * sin + x2 * cos], axis=-1)


def _ret_chunk(S, qkv):
    q, k, v = qkv
    L = q.shape[2]
    lg = _ret_log_decay()
    n = jnp.arange(L, dtype=jnp.float32)
    diff = n[:, None] - n[None, :]
    decay = jnp.where(diff >= 0, jnp.exp(lg[:, None, None] * jnp.maximum(diff, 0.0)), 0.0)
    att = jnp.einsum('bhld,bhmd->bhlm', q, k) * decay
    o = (jnp.einsum('bhlm,bhme->bhle', att, v)
         + jnp.einsum('bhld,bhde->bhle', q, S) * jnp.exp(lg[:, None] * (n + 1.0))[:, :, None])
    kw = k * jnp.exp(lg[:, None] * (L - 1.0 - n))[:, :, None]
    S_new = jnp.exp(lg * L)[:, None, None] * S + jnp.einsum('bhmd,bhme->bhde', kw, v)
    return S_new, o


def _retention(q, k, v, S0):
    b, h, L, _ = q.shape
    if L <= CHUNK:
        return _ret_chunk(S0, (q, k, v))
    nc = L // CHUNK

    def split(t):
        return jnp.moveaxis(t.reshape(b, h, nc, CHUNK, t.shape[-1]), 2, 0)

    S, o = lax.scan(_ret_chunk, S0, (split(q), split(k), split(v)))
    o = jnp.moveaxis(o, 0, 2).reshape(b, h, L, v.shape[-1])
    return S, o


def _mixer(h, conv_state, ret_state, pos, w_in, conv_w, w_out):
    b, l, _ = h.shape
    pts = np.cumsum(IN_SPLITS)[:-1].tolist()
    cb, cc, ch, q, k, v, rg, ga, gb = jnp.split(h @ w_in, pts, axis=-1)
    u = cc * ch
    ext = jnp.concatenate([conv_state.astype(u.dtype), u], axis=1)
    conv = ext[:, 0:l] * conv_w[0] + ext[:, 1:l + 1] * conv_w[1] + ext[:, 2:l + 2] * conv_w[2]
    y_conv = cb * conv
    new_conv = ext[:, -(CONV_W - 1):]
    qh = jnp.transpose(_rope(q.reshape(b, l, RET_HEADS, RET_DK), pos), (0, 2, 1, 3))
    kh = jnp.transpose(_rope(k.reshape(b, l, RET_HEADS, RET_DK), pos) * (RET_DK ** -0.5), (0, 2, 1, 3))
    vh = jnp.transpose(v.reshape(b, l, RET_HEADS, RET_DV).astype(jnp.float32), (0, 2, 1, 3))
    S, o = _retention(qh, kh, vh, ret_state.astype(jnp.float32))
    o = o * lax.rsqrt(jnp.mean(o * o, axis=-1, keepdims=True) + EPS)
    o = jnp.transpose(o, (0, 2, 1, 3)).reshape(b, l, D_RET_V).astype(h.dtype)
    y_ret = jax.nn.silu(rg) * o
    m = jax.nn.sigmoid(ga) * y_conv + jax.nn.sigmoid(gb) * y_ret
    return m @ w_out, new_conv, S.astype(ret_state.dtype)


def _peer(h, w_q, sub_keys, u_tab, v_tab):
    b, l, d = h.shape
    T = b * l
    pad = (-T) % PEER_BLOCK
    blocks = jnp.pad(h.reshape(T, d), ((0, pad), (0, 0))).reshape(-1, PEER_BLOCK, d)
    sk = sub_keys.astype(jnp.float32)

    def blk(hb):
        q = (hb @ w_q).reshape(PEER_BLOCK, PEER_HEADS, 2, D_HALF).astype(jnp.float32)
        s = jnp.einsum('thpd,pkd->thpk', q, sk)
        sv, si = lax.top_k(s, PEER_TOPK)
        cand = (sv[:, :, 0, :, None] + sv[:, :, 1, None, :]).reshape(PEER_BLOCK, PEER_HEADS, PEER_TOPK * PEER_TOPK)
        cidx = (si[:, :, 0, :, None] * N_KEYS + si[:, :, 1, None, :]).reshape(PEER_BLOCK, PEER_HEADS, PEER_TOPK * PEER_TOPK)
        ts, tp = lax.top_k(cand, PEER_TOPK)
        e = jnp.take_along_axis(cidx, tp, axis=-1)
        g = jax.nn.softmax(ts, axis=-1)
        a = jax.nn.gelu(jnp.einsum('thkd,td->thk', u_tab[e], hb).astype(jnp.float32), approximate=False)
        return jnp.einsum('thk,thkd->td', (g * a).astype(hb.dtype), v_tab[e])

    out = lax.map(blk, blocks).reshape(-1, d)[:T]
    return out.reshape(b, l, d)


def _layer(x, conv_state, ret_state, pos, norm1, w_in, conv_w, w_out, norm2, wq, sk, u_tab, v_tab):
    m, cs, rs = _mixer(_rmsnorm(x, norm1), conv_state, ret_state, pos, w_in, conv_w, w_out)
    x = x + m
    x = x + _peer(_rmsnorm(x, norm2), wq, sk, u_tab, v_tab)
    return x, cs, rs


def setup_inputs(seed: int = 0) -> dict:
    key = jax.random.key(seed)
    ks = jax.random.split(key, 15)
    nrm = jax.random.normal
    f32 = jnp.float32
    return {
        'x_prompt': nrm(ks[0], (BATCH, SEQ, D_MODEL), f32),
        'x_sample': nrm(ks[1], (DEC_BATCH, DEC_SEQ, D_MODEL), f32),
        'state_conv': nrm(ks[2], (DEPTH, DEC_BATCH, CONV_W - 1, D_CONV), f32),
        'state_ret': nrm(ks[3], (DEPTH, DEC_BATCH, RET_HEADS, RET_DK, RET_DV), f32) * RET_DK ** -0.5,
        'norm1_g': 1.0 + 0.02 * nrm(ks[4], (DEPTH, D_MODEL), f32),
        'w_in': nrm(ks[5], (DEPTH, D_MODEL, D_IN), f32) * D_MODEL ** -0.5,
        'conv_w': nrm(ks[6], (DEPTH, CONV_W, D_CONV), f32) * CONV_W ** -0.5,
        'w_out': nrm(ks[7], (DEPTH, D_MIX, D_MODEL), f32) * D_MIX ** -0.5,
        'norm2_g': 1.0 + 0.02 * nrm(ks[8], (DEPTH, D_MODEL), f32),
        'peer_wq': nrm(ks[9], (DEPTH, D_MODEL, PEER_HEADS * D_KEY), f32) * D_MODEL ** -0.5,
        'peer_subkeys': nrm(ks[10], (DEPTH, 2, N_KEYS, D_HALF), f32) * D_HALF ** -0.5,
        'peer_u': nrm(ks[11], (DEPTH, N_EXPERTS, D_MODEL), f32) * D_MODEL ** -0.5,
        'peer_v': nrm(ks[12], (DEPTH, N_EXPERTS, D_MODEL), f32) * PEER_HEADS ** -0.5,
        'final_g': 1.0 + 0.02 * nrm(ks[13], (D_MODEL,), f32),
    }


def reference(x_prompt, x_sample, state_conv, state_ret, norm1_g, w_in, conv_w, w_out, norm2_g,
              peer_wq, peer_subkeys, peer_u, peer_v, final_g):
    bp, lp, _ = x_prompt.shape
    ls = x_sample.shape[1]
    pos_p = jnp.arange(lp)
    pos_s = PAST_LEN + jnp.arange(ls)
    zero_conv = jnp.zeros((bp, CONV_W - 1, D_CONV), x_prompt.dtype)
    zero_ret = jnp.zeros((bp, RET_HEADS, RET_DK, RET_DV), x_prompt.dtype)
    xp, xs = x_prompt, x_sample
    cp_l, rp_l, cs_l, rs_l = [], [], [], []
    for i in range(DEPTH):
        ws = (norm1_g[i], w_in[i], conv_w[i], w_out[i], norm2_g[i], peer_wq[i], peer_subkeys[i], peer_u[i], peer_v[i])
        xp, cp, rp = _layer(xp, zero_conv, zero_ret, pos_p, *ws)
        xs, cs, rs = _layer(xs, state_conv[i], state_ret[i], pos_s, *ws)
        cp_l.append(cp)
        rp_l.append(rp)
        cs_l.append(cs)
        rs_l.append(rs)
    y_prompt = _rmsnorm(xp, final_g)
    y_sample = _rmsnorm(xs, final_g)
    new_conv_prompt = jnp.stack(cp_l)
    new_ret_prompt = jnp.stack(rp_l)
    new_conv_sample = jnp.stack(cs_l)
    new_ret_sample = jnp.stack(rs_l)
    return (y_prompt, y_sample, new_conv_prompt, new_ret_prompt, new_conv_sample, new_ret_sample)
```

```python
import functools

import numpy as np
import jax
import jax.numpy as jnp
from jax import lax
from jax.experimental import pallas as pl
from jax.experimental.pallas import tpu as pltpu

F32 = jnp.float32
BF16 = jnp.bfloat16

EPS = 1e-6
ROPE_BASE = 10000.0
PAST_LEN = 1024
RET_HEADS = 4
PEER_HEADS = 8
PEER_TOPK = 16
LOG_DECAY = tuple(float(np.log1p(-np.exp2(np.float32(-5.0 - h)))) for h in range(RET_HEADS))
NEG_INF = float("-inf")
NO_RANK = 127.0

VMEM_LIMIT_BYTES = 56 * 1024 * 1024


def _rmsnorm(x, g):
    return x * lax.rsqrt(jnp.mean(x * x, axis=-1, keepdims=True) + EPS) * g


def _dot(a, b):
    return jnp.dot(a, b, preferred_element_type=F32)


def _mixer_kernel(*refs, tl, d, has_state):
    if has_state:
        (x_ref, cos_ref, sin_ref, cst_ref, rst_ref, g1_ref, win_ref, cw_ref, wout_ref,
         x1_ref, nconv_ref, nret_ref) = refs
    else:
        (x_ref, cos_ref, sin_ref, g1_ref, win_ref, cw_ref, wout_ref,
         x1_ref, nconv_ref, nret_ref) = refs
    dk = d // RET_HEADS
    half = dk // 2

    @pl.when(pl.program_id(1) == 0)
    def _():
        if has_state:
            nconv_ref[...] = cst_ref[...]
            nret_ref[...] = rst_ref[...]
        else:
            nconv_ref[...] = jnp.zeros_like(nconv_ref)
            nret_ref[...] = jnp.zeros_like(nret_ref)

    x = x_ref[0]
    h = _rmsnorm(x, g1_ref[...]).astype(BF16)

    def proj(g):
        return _dot(h, win_ref[:, g * d:(g + 1) * d])

    u = proj(1) * proj(2)
    carry = nconv_ref[0]
    row = lax.broadcasted_iota(jnp.int32, (tl, d), 0)
    u1 = jnp.where(row == 0, carry[1:2], pltpu.roll(u, 1, 0))
    u2 = jnp.where(row == 0, carry[0:1], jnp.where(row == 1, carry[1:2], pltpu.roll(u, 2, 0)))
    cw = cw_ref[...]
    y_conv = proj(0) * (u2 * cw[0:1] + u1 * cw[1:2] + u * cw[2:3])
    nconv_ref[0] = u[tl - 2:tl, :]
    gate_a = jax.nn.sigmoid(proj(7))
    m_conv = gate_a * y_conv

    cos = cos_ref[...]
    sin = sin_ref[...]
    q = proj(3)
    k = proj(4)
    v = proj(5)
    n_idx = lax.broadcasted_iota(jnp.int32, (tl, tl), 0)
    m_idx = lax.broadcasted_iota(jnp.int32, (tl, tl), 1)
    diff = (n_idx - m_idx).astype(F32)
    rowf = lax.broadcasted_iota(jnp.int32, (tl, dk), 0).astype(F32)

    def rope(t):
        t1 = t[:, :half]
        t2 = t[:, half:]
        return jnp.concatenate([t1 * cos - t2 * sin, t1 * sin + t2 * cos], axis=-1)

    outs = []
    for hh in range(RET_HEADS):
        lg = LOG_DECAY[hh]
        sl = slice(hh * dk, (hh + 1) * dk)
        qr = rope(q[:, sl])
        kr = rope(k[:, sl]) * (dk ** -0.5)
        vb = v[:, sl].astype(BF16)
        qb = qr.astype(BF16)
        decay = jnp.where(diff >= 0, jnp.exp(lg * jnp.maximum(diff, 0.0)), 0.0)
        att = lax.dot_general(qb, kr.astype(BF16), (((1,), (1,)), ((), ())),
                              preferred_element_type=F32) * decay
        s_old = nret_ref[0, hh]
        o = _dot(att.astype(BF16), vb) + _dot(qb, s_old.astype(BF16)) * jnp.exp(lg * (rowf + 1.0))
        kw = (kr * jnp.exp(lg * (tl - 1.0 - rowf))).astype(BF16)
        nret_ref[0, hh] = float(np.exp(lg * tl)) * s_old + lax.dot_general(
            kw, vb, (((0,), (0,)), ((), ())), preferred_element_type=F32)
        outs.append(o * lax.rsqrt(jnp.mean(o * o, axis=-1, keepdims=True) + EPS))
    o_all = jnp.concatenate(outs, axis=-1)
    rg = proj(6)
    y_ret = rg * jax.nn.sigmoid(rg) * o_all
    m = m_conv + jax.nn.sigmoid(proj(8)) * y_ret
    x1_ref[0] = x + _dot(m.astype(BF16), wout_ref[...])


def _mixer(x, conv_state, ret_state, pos0, g1, win_bf, cw, wout_bf, *, tl):
    b, l, d = x.shape
    dk = d // RET_HEADS
    half = dk // 2
    has_state = conv_state is not None
    inv = ROPE_BASE ** (-jnp.arange(half, dtype=F32) / half)
    ang = (pos0 + jnp.arange(l)).astype(F32)[:, None] * inv[None, :]
    cos, sin = jnp.cos(ang), jnp.sin(ang)

    const = lambda *shape: pl.BlockSpec(shape, lambda i, j: (0,) * len(shape))
    in_specs = [pl.BlockSpec((1, tl, d), lambda i, j: (i, j, 0)),
                pl.BlockSpec((tl, half), lambda i, j: (j, 0)),
                pl.BlockSpec((tl, half), lambda i, j: (j, 0))]
    args = [x, cos, sin]
    if has_state:
        in_specs += [pl.BlockSpec((1, 2, d), lambda i, j: (i, 0, 0)),
                     pl.BlockSpec((1, RET_HEADS, dk, dk), lambda i, j: (i, 0, 0, 0))]
        args += [conv_state, ret_state]
    in_specs += [const(1, d), const(d, win_bf.shape[1]), const(3, d), const(d, d)]
    args += [g1.reshape(1, d), win_bf, cw, wout_bf]
    return pl.pallas_call(
        functools.partial(_mixer_kernel, tl=tl, d=d, has_state=has_state),
        grid=(b, l // tl),
        in_specs=in_specs,
        out_specs=[pl.BlockSpec((1, tl, d), lambda i, j: (i, j, 0)),
                   pl.BlockSpec((1, 2, d), lambda i, j: (i, 0, 0)),
                   pl.BlockSpec((1, RET_HEADS, dk, dk), lambda i, j: (i, 0, 0, 0))],
        out_shape=[jax.ShapeDtypeStruct((b, l, d), F32),
                   jax.ShapeDtypeStruct((b, 2, d), F32),
                   jax.ShapeDtypeStruct((b, RET_HEADS, dk, dk), F32)],
        compiler_params=pltpu.CompilerParams(
            dimension_semantics=("arbitrary", "arbitrary"), vmem_limit_bytes=VMEM_LIMIT_BYTES),
        name="mixer",
    )(*args)


def _top16(s):
    cur = s
    rank = jnp.full(s.shape, NO_RANK, F32)
    vals = []
    for r in range(PEER_TOPK):
        mx = jnp.max(cur, axis=0, keepdims=True)
        hit = cur == mx
        rank = jnp.where(hit, float(r), rank)
        cur = jnp.where(hit, NEG_INF, cur)
        vals.append(mx)
    return vals, rank


def _peer_sel_kernel(x1_ref, g2_ref, wqt_ref, sk_ref,
                     h2t_ref, rank1_ref, p1_ref, n0_ref, p0_ref, qt_ref, *, n_keys, d_half):
    h2 = _rmsnorm(x1_ref[...], g2_ref[...])
    h2t = h2.T.astype(BF16)
    h2t_ref[...] = h2t
    qt_ref[...] = _dot(wqt_ref[...], h2t)
    sk0 = sk_ref[0]
    sk1 = sk_ref[1]

    def head(hd, _):
        base = pl.multiple_of(hd * (2 * d_half), 2 * d_half)
        s0 = _dot(sk0, qt_ref[pl.ds(base, d_half), :].astype(BF16))
        s1 = _dot(sk1, qt_ref[pl.ds(base + d_half, d_half), :].astype(BF16))
        vals0, rank0 = _top16(s0)
        vals1, rank1 = _top16(s1)
        a1 = jnp.concatenate(vals1, axis=0)
        blocks = []
        for r0 in range(PEER_TOPK):
            rows = 16 if r0 == 0 else 8
            blk = a1[:rows] + vals0[r0]
            n1 = PEER_TOPK // (r0 + 1)
            if n1 < rows:
                blk = jnp.where(lax.broadcasted_iota(jnp.int32, blk.shape, 0) < n1, blk, NEG_INF)
            blocks.append(blk)
        cand = jnp.concatenate(blocks, axis=0)
        cur = cand
        cmax = None
        z = None
        for r in range(PEER_TOPK):
            mx = jnp.max(cur, axis=0, keepdims=True)
            if r == 0:
                cmax = mx
                z = jnp.ones_like(mx)
            else:
                z = z + jnp.exp(mx - cmax)
            cur = jnp.where(cur == mx, NEG_INF, cur)
        sel = (cand >= mx).astype(F32)
        n0 = jnp.zeros_like(s0)
        off = 0
        for r0 in range(PEER_TOPK):
            rows = 16 if r0 == 0 else 8
            cnt = jnp.sum(sel[off:off + rows], axis=0, keepdims=True)
            off += rows
            n0 = jnp.where(rank0 == float(r0), cnt, n0)
        n0_ref[hd] = n0
        p0_ref[hd] = jnp.exp(s0 - vals0[0])
        rank1_ref[hd] = rank1.astype(BF16)
        p1_ref[hd] = (jnp.exp(s1 - vals1[0]) / z).astype(BF16)
        return 0

    lax.fori_loop(0, PEER_HEADS, head, 0)


def _peer_sel(x1, g2, wqt_bf, sk_bf, *, tt):
    t, d = x1.shape
    _, n_keys, d_half = sk_bf.shape
    dq = wqt_bf.shape[0]
    tab = lambda dt: jax.ShapeDtypeStruct((PEER_HEADS, n_keys, t), dt)
    tab_spec = pl.BlockSpec((PEER_HEADS, n_keys, tt), lambda i: (0, 0, i))
    return pl.pallas_call(
        functools.partial(_peer_sel_kernel, n_keys=n_keys, d_half=d_half),
        grid=(t // tt,),
        in_specs=[pl.BlockSpec((tt, d), lambda i: (i, 0)),
                  pl.BlockSpec((1, d), lambda i: (0, 0)),
                  pl.BlockSpec((dq, d), lambda i: (0, 0)),
                  pl.BlockSpec((2, n_keys, d_half), lambda i: (0, 0, 0))],
        out_specs=[pl.BlockSpec((d, tt), lambda i: (0, i)), tab_spec, tab_spec, tab_spec, tab_spec],
        out_shape=[jax.ShapeDtypeStruct((d, t), BF16), tab(BF16), tab(BF16), tab(F32), tab(F32)],
        scratch_shapes=[pltpu.VMEM((dq, tt), F32)],
        compiler_params=pltpu.CompilerParams(
            dimension_semantics=("arbitrary",), vmem_limit_bytes=VMEM_LIMIT_BYTES),
        name="peer_sel",
    )(x1, g2.reshape(1, d), wqt_bf, sk_bf)


def _peer_main_kernel(h2t_ref, rank1_ref, p1_ref, n0_ref, p0_ref, u_ref, vt_ref, x1_ref, gf_ref,
                      y_ref, acc_ref, *, n_keys, te, final_norm):
    e = pl.program_id(1)

    @pl.when(e == 0)
    def _():
        acc_ref[...] = jnp.zeros_like(acc_ref)

    tt = h2t_ref.shape[1]
    a = _dot(u_ref[...], h2t_ref[...]).astype(BF16)
    act = (0.5 * a) * (1.0 + lax.erf(a * np.float32(2.0 ** -0.5).astype(BF16)))
    gates = []
    for ii in range(te // n_keys):
        i = e * (te // n_keys) + ii
        w = jnp.zeros((n_keys, tt), BF16)
        for hd in range(PEER_HEADS):
            n_row = jnp.broadcast_to(n0_ref[hd, pl.ds(i, 1), :], (16, tt)).astype(BF16)
            p_row = jnp.broadcast_to(p0_ref[hd, pl.ds(i, 1), :], (16, tt)).astype(BF16)
            n_b = jnp.broadcast_to(n_row[None], (n_keys // 16, 16, tt)).reshape(n_keys, tt)
            p_b = jnp.broadcast_to(p_row[None], (n_keys // 16, 16, tt)).reshape(n_keys, tt)
            w = w + jnp.where(rank1_ref[hd] < n_b, p1_ref[hd], jnp.zeros((), BF16)) * p_b
        gates.append(w)
    w_all = jnp.concatenate(gates, axis=0)
    acc_ref[...] += _dot(vt_ref[...], w_all * act)

    @pl.when(e == pl.num_programs(1) - 1)
    def _():
        out = x1_ref[...] + acc_ref[...].T
        if final_norm:
            out = _rmsnorm(out, gf_ref[...])
        y_ref[...] = out


def _peer_main(x1, h2t, rank1, p1, n0, p0, u_bf, vt_bf, gf, *, tt, te, final_norm):
    t, d = x1.shape
    n_exp = u_bf.shape[0]
    n_keys = rank1.shape[1]
    tab_spec = pl.BlockSpec((PEER_HEADS, n_keys, tt), lambda i, e: (0, 0, i))
    return pl.pallas_call(
        functools.partial(_peer_main_kernel, n_keys=n_keys, te=te, final_norm=final_norm),
        grid=(t // tt, n_exp // te),
        in_specs=[pl.BlockSpec((d, tt), lambda i, e: (0, i)),
                  tab_spec, tab_spec, tab_spec, tab_spec,
                  pl.BlockSpec((te, d), lambda i, e: (e, 0)),
                  pl.BlockSpec((d, te), lambda i, e: (0, e)),
                  pl.BlockSpec((tt, d), lambda i, e: (i, 0)),
                  pl.BlockSpec((1, d), lambda i, e: (0, 0))],
        out_specs=pl.BlockSpec((tt, d), lambda i, e: (i, 0)),
        out_shape=jax.ShapeDtypeStruct((t, d), F32),
        scratch_shapes=[pltpu.VMEM((d, tt), F32)],
        compiler_params=pltpu.CompilerParams(
            dimension_semantics=("arbitrary", "arbitrary"), vmem_limit_bytes=VMEM_LIMIT_BYTES),
        name="peer_main",
    )(h2t, rank1, p1, n0, p0, u_bf, vt_bf, x1, gf.reshape(1, d))


def _pick_tile(n, cap):
    tile = min(n, cap)
    assert n % tile == 0, (n, tile)
    return tile


def kernel(x_prompt, x_sample, state_conv, state_ret, norm1_g, w_in, conv_w, w_out, norm2_g,
           peer_wq, peer_subkeys, peer_u, peer_v, final_g):
    depth = w_in.shape[0]
    d = x_prompt.shape[-1]
    assert peer_wq.shape[-1] == PEER_HEADS * 2 * peer_subkeys.shape[-1]
    assert peer_u.shape[1] == peer_subkeys.shape[2] ** 2
    streams = [(x_prompt, None, None, 0), (x_sample, state_conv, state_ret, PAST_LEN)]
    xs = [s[0] for s in streams]
    conv_out = [[] for _ in streams]
    ret_out = [[] for _ in streams]
    for layer in range(depth):
        win_bf = w_in[layer].astype(BF16)
        wout_bf = w_out[layer].astype(BF16)
        wqt_bf = peer_wq[layer].T.astype(BF16)
        sk_bf = peer_subkeys[layer].astype(BF16)
        u_bf = peer_u[layer].astype(BF16)
        vt_bf = peer_v[layer].T.astype(BF16)
        for si, (_, cst, rst, pos0) in enumerate(streams):
            x = xs[si]
            b, l, _ = x.shape
            x1, nconv, nret = _mixer(
                x, None if cst is None else cst[layer], None if rst is None else rst[layer], pos0,
                norm1_g[layer], win_bf, conv_w[layer], wout_bf, tl=_pick_tile(l, 256))
            conv_out[si].append(nconv)
            ret_out[si].append(nret)
            x1 = x1.reshape(b * l, d)
            h2t, rank1, p1, n0, p0 = _peer_sel(x1, norm2_g[layer], wqt_bf, sk_bf,
                                               tt=_pick_tile(b * l, 256))
            y = _peer_main(x1, h2t, rank1, p1, n0, p0, u_bf, vt_bf, final_g,
                           tt=_pick_tile(b * l, 512), te=512, final_norm=layer == depth - 1)
            xs[si] = y.reshape(b, l, d)
    return (xs[0], xs[1], jnp.stack(conv_out[0]), jnp.stack(ret_out[0]),
            jnp.stack(conv_out[1]), jnp.stack(ret_out[1]))
```

```python
import functools

import numpy as np
import jax
import jax.numpy as jnp
from jax import lax
from jax.experimental import pallas as pl
from jax.experimental.pallas import tpu as pltpu

F32 = jnp.float32
BF16 = jnp.bfloat16

EPS = 1e-6
ROPE_BASE = 10000.0
PAST_LEN = 1024
RET_HEADS = 4
PEER_HEADS = 8
PEER_TOPK = 16
LOG_DECAY = tuple(float(np.log1p(-np.exp2(np.float32(-5.0 - h)))) for h in range(RET_HEADS))
NEG_INF = float("-inf")
NO_RANK = 127.0

VMEM_LIMIT_BYTES = 56 * 1024 * 1024
LANES = 128


def _rmsnorm(x, g):
    return x * lax.rsqrt(jnp.mean(x * x, axis=-1, keepdims=True) + EPS) * g


def _dot(a, b):
    return jnp.dot(a, b, preferred_element_type=F32)


def _mixer_kernel(*refs, tl, d, has_state):
    if has_state:
        (x_ref, cos_ref, sin_ref, cst_ref, rst_ref, g1_ref, win_ref, cw_ref, wout_ref,
         x1_ref, nconv_ref, nret_ref) = refs
    else:
        (x_ref, cos_ref, sin_ref, g1_ref, win_ref, cw_ref, wout_ref,
         x1_ref, nconv_ref, nret_ref) = refs
    dk = d // RET_HEADS
    half = dk // 2

    @pl.when(pl.program_id(1) == 0)
    def _():
        if has_state:
            nconv_ref[...] = cst_ref[...]
            nret_ref[...] = rst_ref[...]
        else:
            nconv_ref[...] = jnp.zeros_like(nconv_ref)
            nret_ref[...] = jnp.zeros_like(nret_ref)

    x = x_ref[0]
    h = _rmsnorm(x, g1_ref[...]).astype(BF16)

    def proj(g):
        return _dot(h, win_ref[:, g * d:(g + 1) * d])

    u = proj(1) * proj(2)
    carry = nconv_ref[0]
    row = lax.broadcasted_iota(jnp.int32, (tl, d), 0)
    u1 = jnp.where(row == 0, carry[1:2], pltpu.roll(u, 1, 0))
    u2 = jnp.where(row == 0, carry[0:1], jnp.where(row == 1, carry[1:2], pltpu.roll(u, 2, 0)))
    cw = cw_ref[...]
    y_conv = proj(0) * (u2 * cw[0:1] + u1 * cw[1:2] + u * cw[2:3])
    nconv_ref[0] = u[tl - 2:tl, :]
    gate_a = jax.nn.sigmoid(proj(7))
    m_conv = gate_a * y_conv

    cos = cos_ref[...]
    sin = sin_ref[...]
    q = proj(3)
    k = proj(4)
    v = proj(5)
    n_idx = lax.broadcasted_iota(jnp.int32, (tl, tl), 0)
    m_idx = lax.broadcasted_iota(jnp.int32, (tl, tl), 1)
    diff = (n_idx - m_idx).astype(F32)
    rowf = lax.broadcasted_iota(jnp.int32, (tl, dk), 0).astype(F32)

    def rope(t):
        t1 = t[:, :half]
        t2 = t[:, half:]
        return jnp.concatenate([t1 * cos - t2 * sin, t1 * sin + t2 * cos], axis=-1)

    outs = []
    for hh in range(RET_HEADS):
        lg = LOG_DECAY[hh]
        sl = slice(hh * dk, (hh + 1) * dk)
        qr = rope(q[:, sl])
        kr = rope(k[:, sl]) * (dk ** -0.5)
        vb = v[:, sl].astype(BF16)
        qb = qr.astype(BF16)
        decay = jnp.where(diff >= 0, jnp.exp(lg * jnp.maximum(diff, 0.0)), 0.0)
        att = lax.dot_general(qb, kr.astype(BF16), (((1,), (1,)), ((), ())),
                              preferred_element_type=F32) * decay
        s_old = nret_ref[0, hh]
        o = _dot(att.astype(BF16), vb) + _dot(qb, s_old.astype(BF16)) * jnp.exp(lg * (rowf + 1.0))
        kw = (kr * jnp.exp(lg * (tl - 1.0 - rowf))).astype(BF16)
        nret_ref[0, hh] = float(np.exp(lg * tl)) * s_old + lax.dot_general(
            kw, vb, (((0,), (0,)), ((), ())), preferred_element_type=F32)
        outs.append(o * lax.rsqrt(jnp.mean(o * o, axis=-1, keepdims=True) + EPS))
    o_all = jnp.concatenate(outs, axis=-1)
    rg = proj(6)
    y_ret = rg * jax.nn.sigmoid(rg) * o_all
    m = m_conv + jax.nn.sigmoid(proj(8)) * y_ret
    x1_ref[0] = x + _dot(m.astype(BF16), wout_ref[...])


def _mixer(x, conv_state, ret_state, pos0, g1, win_bf, cw, wout_bf, *, tl):
    b, l, d = x.shape
    dk = d // RET_HEADS
    half = dk // 2
    has_state = conv_state is not None
    inv = ROPE_BASE ** (-jnp.arange(half, dtype=F32) / half)
    ang = (pos0 + jnp.arange(l)).astype(F32)[:, None] * inv[None, :]
    cos, sin = jnp.cos(ang), jnp.sin(ang)

    const = lambda *shape: pl.BlockSpec(shape, lambda i, j: (0,) * len(shape))
    in_specs = [pl.BlockSpec((1, tl, d), lambda i, j: (i, j, 0)),
                pl.BlockSpec((tl, half), lambda i, j: (j, 0)),
                pl.BlockSpec((tl, half), lambda i, j: (j, 0))]
    args = [x, cos, sin]
    if has_state:
        in_specs += [pl.BlockSpec((1, 2, d), lambda i, j: (i, 0, 0)),
                     pl.BlockSpec((1, RET_HEADS, dk, dk), lambda i, j: (i, 0, 0, 0))]
        args += [conv_state, ret_state]
    in_specs += [const(1, d), const(d, win_bf.shape[1]), const(3, d), const(d, d)]
    args += [g1.reshape(1, d), win_bf, cw, wout_bf]
    return pl.pallas_call(
        functools.partial(_mixer_kernel, tl=tl, d=d, has_state=has_state),
        grid=(b, l // tl),
        in_specs=in_specs,
        out_specs=[pl.BlockSpec((1, tl, d), lambda i, j: (i, j, 0)),
                   pl.BlockSpec((1, 2, d), lambda i, j: (i, 0, 0)),
                   pl.BlockSpec((1, RET_HEADS, dk, dk), lambda i, j: (i, 0, 0, 0))],
        out_shape=[jax.ShapeDtypeStruct((b, l, d), F32),
                   jax.ShapeDtypeStruct((b, 2, d), F32),
                   jax.ShapeDtypeStruct((b, RET_HEADS, dk, dk), F32)],
        compiler_params=pltpu.CompilerParams(
            dimension_semantics=("arbitrary", "arbitrary"), vmem_limit_bytes=VMEM_LIMIT_BYTES),
        name="mixer",
    )(*args)


def _top16(s):
    cur = s
    rank = jnp.full(s.shape, NO_RANK, F32)
    vals = []
    for r in range(PEER_TOPK):
        mx = jnp.max(cur, axis=0, keepdims=True)
        hit = cur == mx
        rank = jnp.where(hit, float(r), rank)
        cur = jnp.where(hit, NEG_INF, cur)
        vals.append(mx)
    return vals, rank


def _peer_sel_kernel(x1_ref, g2_ref, wqt_ref, sk_ref,
                     h2t_ref, rank1_ref, p1_ref, n0_ref, p0_ref, qt_ref, *, n_keys, d_half):
    h2 = _rmsnorm(x1_ref[...], g2_ref[...])
    h2t = h2.T.astype(BF16)
    h2t_ref[...] = h2t
    qt_ref[...] = _dot(wqt_ref[...], h2t)
    sk0 = sk_ref[0]
    sk1 = sk_ref[1]

    def head(hd, _):
        base = pl.multiple_of(hd * (2 * d_half), 2 * d_half)
        s0 = _dot(sk0, qt_ref[pl.ds(base, d_half), :].astype(BF16))
        s1 = _dot(sk1, qt_ref[pl.ds(base + d_half, d_half), :].astype(BF16))
        vals0, rank0 = _top16(s0)
        vals1, rank1 = _top16(s1)
        a1 = jnp.concatenate(vals1, axis=0)
        blocks = []
        for r0 in range(PEER_TOPK):
            rows = 16 if r0 == 0 else 8
            blk = a1[:rows] + vals0[r0]
            n1 = PEER_TOPK // (r0 + 1)
            if n1 < rows:
                blk = jnp.where(lax.broadcasted_iota(jnp.int32, blk.shape, 0) < n1, blk, NEG_INF)
            blocks.append(blk)
        cand = jnp.concatenate(blocks, axis=0)
        cur = cand
        cmax = None
        z = None
        for r in range(PEER_TOPK):
            mx = jnp.max(cur, axis=0, keepdims=True)
            if r == 0:
                cmax = mx
                z = jnp.ones_like(mx)
            else:
                z = z + jnp.exp(mx - cmax)
            cur = jnp.where(cur == mx, NEG_INF, cur)
        sel = (cand >= mx).astype(F32)
        n0 = jnp.zeros_like(s0)
        off = 0
        for r0 in range(PEER_TOPK):
            rows = 16 if r0 == 0 else 8
            cnt = jnp.sum(sel[off:off + rows], axis=0, keepdims=True)
            off += rows
            n0 = jnp.where(rank0 == float(r0), cnt, n0)
        n0_ref[hd] = n0
        p0_ref[hd] = jnp.exp(s0 - vals0[0])
        rank1_ref[hd] = rank1.astype(BF16)
        p1_ref[hd] = (jnp.exp(s1 - vals1[0]) / z).astype(BF16)
        return 0

    lax.fori_loop(0, PEER_HEADS, head, 0)


def _peer_sel(x1, g2, wqt_bf, sk_bf, *, tt):
    t, d = x1.shape
    _, n_keys, d_half = sk_bf.shape
    dq = wqt_bf.shape[0]
    tab = lambda dt: jax.ShapeDtypeStruct((PEER_HEADS, n_keys, t), dt)
    tab_spec = pl.BlockSpec((PEER_HEADS, n_keys, tt), lambda i: (0, 0, i))
    return pl.pallas_call(
        functools.partial(_peer_sel_kernel, n_keys=n_keys, d_half=d_half),
        grid=(t // tt,),
        in_specs=[pl.BlockSpec((tt, d), lambda i: (i, 0)),
                  pl.BlockSpec((1, d), lambda i: (0, 0)),
                  pl.BlockSpec((dq, d), lambda i: (0, 0)),
                  pl.BlockSpec((2, n_keys, d_half), lambda i: (0, 0, 0))],
        out_specs=[pl.BlockSpec((d, tt), lambda i: (0, i)), tab_spec, tab_spec, tab_spec, tab_spec],
        out_shape=[jax.ShapeDtypeStruct((d, t), BF16), tab(BF16), tab(BF16), tab(F32), tab(F32)],
        scratch_shapes=[pltpu.VMEM((dq, tt), F32)],
        compiler_params=pltpu.CompilerParams(
            dimension_semantics=("arbitrary",), vmem_limit_bytes=VMEM_LIMIT_BYTES),
        name="peer_sel",
    )(x1, g2.reshape(1, d), wqt_bf, sk_bf)


def _peer_main_kernel(h2t_ref, rank1_ref, p1_ref, n0_ref, p0_ref, u_ref, vt_ref, x1_ref, gf_ref,
                      y_ref, acc_ref, a_ref, g_ref, *, n_keys, te, n_eblk, final_norm):
    s = pl.program_id(0)
    n_pairs = pl.num_programs(0) - 2
    tt = h2t_ref.shape[1]
    slot0 = s % 2
    slot1 = (s + 1) % 2

    @pl.when(s == 0)
    def _():
        a_ref[...] = jnp.zeros_like(a_ref)
        g_ref[...] = jnp.zeros_like(g_ref)

    pair3 = jnp.clip(s - 2, 0, n_pairs - 1)
    e3 = pair3 % n_eblk

    @pl.when(e3 == 0)
    def _():
        acc_ref[...] = jnp.zeros_like(acc_ref)

    a_ref[slot0] = _dot(u_ref[...], h2t_ref[...]).astype(BF16)

    e2 = jnp.clip(s - 1, 0, n_pairs - 1) % n_eblk
    for ii in range(te // n_keys):
        i = e2 * (te // n_keys) + ii
        rows = slice(ii * n_keys, (ii + 1) * n_keys)
        w = jnp.zeros((n_keys, tt), BF16)
        for hd in range(PEER_HEADS):
            n_row = jnp.broadcast_to(n0_ref[hd, pl.ds(i, 1), :], (16, tt)).astype(BF16)
            p_row = jnp.broadcast_to(p0_ref[hd, pl.ds(i, 1), :], (16, tt)).astype(BF16)
            n_b = jnp.broadcast_to(n_row[None], (n_keys // 16, 16, tt)).reshape(n_keys, tt)
            p_b = jnp.broadcast_to(p_row[None], (n_keys // 16, 16, tt)).reshape(n_keys, tt)
            w = w + jnp.where(rank1_ref[hd] < n_b, p1_ref[hd], jnp.zeros((), BF16)) * p_b
        a = a_ref[slot1, rows, :]
        act = (0.5 * a) * (1.0 + lax.erf(a * np.float32(2.0 ** -0.5).astype(BF16)))
        g_ref[slot1, rows, :] = w * act

    acc_ref[...] += _dot(vt_ref[...], g_ref[slot0])

    @pl.when(jnp.logical_and(e3 == n_eblk - 1, s >= 2))
    def _():
        out = x1_ref[...] + acc_ref[...].T
        if final_norm:
            out = _rmsnorm(out, gf_ref[...])
        y_ref[...] = out


def _peer_main(x1, h2t, rank1, p1, n0, p0, u_bf, vt_bf, gf, *, tt, te, final_norm):
    t, d = x1.shape
    n_exp = u_bf.shape[0]
    n_keys = rank1.shape[1]
    n_eblk = n_exp // te
    n_pairs = (t // tt) * n_eblk

    def pair(s, lag):
        p = jnp.clip(s - lag, 0, n_pairs - 1)
        return p // n_eblk, p % n_eblk

    tab_spec = pl.BlockSpec((PEER_HEADS, n_keys, tt), lambda s: (0, 0, pair(s, 1)[0]))
    return pl.pallas_call(
        functools.partial(_peer_main_kernel, n_keys=n_keys, te=te, n_eblk=n_eblk,
                          final_norm=final_norm),
        grid=(n_pairs + 2,),
        in_specs=[pl.BlockSpec((d, tt), lambda s: (0, pair(s, 0)[0])),
                  tab_spec, tab_spec, tab_spec, tab_spec,
                  pl.BlockSpec((te, d), lambda s: (pair(s, 0)[1], 0)),
                  pl.BlockSpec((d, te), lambda s: (0, pair(s, 2)[1])),
                  pl.BlockSpec((tt, d), lambda s: (pair(s, 2)[0], 0)),
                  pl.BlockSpec((1, d), lambda s: (0, 0))],
        out_specs=pl.BlockSpec((tt, d), lambda s: (pair(s, 2)[0], 0)),
        out_shape=jax.ShapeDtypeStruct((t, d), F32),
        scratch_shapes=[pltpu.VMEM((d, tt), F32), pltpu.VMEM((2, te, tt), BF16),
                        pltpu.VMEM((2, te, tt), BF16)],
        compiler_params=pltpu.CompilerParams(
            dimension_semantics=("arbitrary",), vmem_limit_bytes=VMEM_LIMIT_BYTES),
        name="peer_main",
    )(h2t, rank1, p1, n0, p0, u_bf, vt_bf, x1, gf.reshape(1, d))


def _pick_tile(n, cap):
    tile = min(n, cap)
    assert n % tile == 0, (n, tile)
    return tile


def kernel(x_prompt, x_sample, state_conv, state_ret, norm1_g, w_in, conv_w, w_out, norm2_g,
           peer_wq, peer_subkeys, peer_u, peer_v, final_g):
    depth = w_in.shape[0]
    d = x_prompt.shape[-1]
    assert peer_wq.shape[-1] == PEER_HEADS * 2 * peer_subkeys.shape[-1]
    assert peer_u.shape[1] == peer_subkeys.shape[2] ** 2
    streams = [(x_prompt, None, None, 0), (x_sample, state_conv, state_ret, PAST_LEN)]
    xs = [s[0] for s in streams]
    conv_out = [[] for _ in streams]
    ret_out = [[] for _ in streams]
    for layer in range(depth):
        win_bf = w_in[layer].astype(BF16)
        wout_bf = w_out[layer].astype(BF16)
        wqt_bf = peer_wq[layer].T.astype(BF16)
        sk_bf = peer_subkeys[layer].astype(BF16)
        u_bf = peer_u[layer].astype(BF16)
        vt_bf = peer_v[layer].T.astype(BF16)
        for si, (_, cst, rst, pos0) in enumerate(streams):
            x = xs[si]
            b, l, _ = x.shape
            x1, nconv, nret = _mixer(
                x, None if cst is None else cst[layer], None if rst is None else rst[layer], pos0,
                norm1_g[layer], win_bf, conv_w[layer], wout_bf, tl=_pick_tile(l, 256))
            conv_out[si].append(nconv)
            ret_out[si].append(nret)
            x1 = x1.reshape(b * l, d)
            h2t, rank1, p1, n0, p0 = _peer_sel(x1, norm2_g[layer], wqt_bf, sk_bf,
                                               tt=_pick_tile(b * l, 256))
            y = _peer_main(x1, h2t, rank1, p1, n0, p0, u_bf, vt_bf, final_g,
                           tt=_pick_tile(b * l, 512), te=1024, final_norm=layer == depth - 1)
            xs[si] = y.reshape(b, l, d)
    return (xs[0], xs[1], jnp.stack(conv_out[0]), jnp.stack(ret_out[0]),
            jnp.stack(conv_out[1]), jnp.stack(ret_out[1]))
```

```python
import functools

import numpy as np
import jax
import jax.numpy as jnp
from jax import lax
from jax.experimental import pallas as pl
from jax.experimental.pallas import tpu as pltpu

F32 = jnp.float32
BF16 = jnp.bfloat16

EPS = 1e-6
ROPE_BASE = 10000.0
PAST_LEN = 1024
RET_HEADS = 4
PEER_HEADS = 8
PEER_TOPK = 16
LOG_DECAY = tuple(float(np.log1p(-np.exp2(np.float32(-5.0 - h)))) for h in range(RET_HEADS))
NEG_INF = float("-inf")
NO_RANK = 127.0

VMEM_LIMIT_BYTES = 56 * 1024 * 1024
LANES = 128


def _rmsnorm(x, g):
    return x * lax.rsqrt(jnp.mean(x * x, axis=-1, keepdims=True) + EPS) * g


def _dot(a, b):
    return jnp.dot(a, b, preferred_element_type=F32)


def _mixer_kernel(*refs, tl, d, has_state):
    if has_state:
        (x_ref, cos_ref, sin_ref, cst_ref, rst_ref, g1_ref, win_ref, cw_ref, wout_ref,
         x1_ref, nconv_ref, nret_ref) = refs
    else:
        (x_ref, cos_ref, sin_ref, g1_ref, win_ref, cw_ref, wout_ref,
         x1_ref, nconv_ref, nret_ref) = refs
    dk = d // RET_HEADS
    half = dk // 2

    @pl.when(pl.program_id(1) == 0)
    def _():
        if has_state:
            nconv_ref[...] = cst_ref[...]
            nret_ref[...] = rst_ref[...]
        else:
            nconv_ref[...] = jnp.zeros_like(nconv_ref)
            nret_ref[...] = jnp.zeros_like(nret_ref)

    x = x_ref[0]
    h = _rmsnorm(x, g1_ref[...]).astype(BF16)

    def proj(g):
        return _dot(h, win_ref[:, g * d:(g + 1) * d])

    u = proj(1) * proj(2)
    carry = nconv_ref[0]
    row = lax.broadcasted_iota(jnp.int32, (tl, d), 0)
    u1 = jnp.where(row == 0, carry[1:2], pltpu.roll(u, 1, 0))
    u2 = jnp.where(row == 0, carry[0:1], jnp.where(row == 1, carry[1:2], pltpu.roll(u, 2, 0)))
    cw = cw_ref[...]
    y_conv = proj(0) * (u2 * cw[0:1] + u1 * cw[1:2] + u * cw[2:3])
    nconv_ref[0] = u[tl - 2:tl, :]
    gate_a = jax.nn.sigmoid(proj(7))
    m_conv = gate_a * y_conv

    cos = cos_ref[...]
    sin = sin_ref[...]
    q = proj(3)
    k = proj(4)
    v = proj(5)
    n_idx = lax.broadcasted_iota(jnp.int32, (tl, tl), 0)
    m_idx = lax.broadcasted_iota(jnp.int32, (tl, tl), 1)
    diff = (n_idx - m_idx).astype(F32)
    rowf = lax.broadcasted_iota(jnp.int32, (tl, dk), 0).astype(F32)

    def rope(t):
        t1 = t[:, :half]
        t2 = t[:, half:]
        return jnp.concatenate([t1 * cos - t2 * sin, t1 * sin + t2 * cos], axis=-1)

    outs = []
    for hh in range(RET_HEADS):
        lg = LOG_DECAY[hh]
        sl = slice(hh * dk, (hh + 1) * dk)
        qr = rope(q[:, sl])
        kr = rope(k[:, sl]) * (dk ** -0.5)
        vb = v[:, sl].astype(BF16)
        qb = qr.astype(BF16)
        decay = jnp.where(diff >= 0, jnp.exp(lg * jnp.maximum(diff, 0.0)), 0.0)
        att = lax.dot_general(qb, kr.astype(BF16), (((1,), (1,)), ((), ())),
                              preferred_element_type=F32) * decay
        s_old = nret_ref[0, hh]
        o = _dot(att.astype(BF16), vb) + _dot(qb, s_old.astype(BF16)) * jnp.exp(lg * (rowf + 1.0))
        kw = (kr * jnp.exp(lg * (tl - 1.0 - rowf))).astype(BF16)
        nret_ref[0, hh] = float(np.exp(lg * tl)) * s_old + lax.dot_general(
            kw, vb, (((0,), (0,)), ((), ())), preferred_element_type=F32)
        outs.append(o * lax.rsqrt(jnp.mean(o * o, axis=-1, keepdims=True) + EPS))
    o_all = jnp.concatenate(outs, axis=-1)
    rg = proj(6)
    y_ret = rg * jax.nn.sigmoid(rg) * o_all
    m = m_conv + jax.nn.sigmoid(proj(8)) * y_ret
    x1_ref[0] = x + _dot(m.astype(BF16), wout_ref[...])


def _mixer(x, conv_state, ret_state, pos0, g1, win_bf, cw, wout_bf, *, tl):
    b, l, d = x.shape
    dk = d // RET_HEADS
    half = dk // 2
    has_state = conv_state is not None
    inv = ROPE_BASE ** (-jnp.arange(half, dtype=F32) / half)
    ang = (pos0 + jnp.arange(l)).astype(F32)[:, None] * inv[None, :]
    cos, sin = jnp.cos(ang), jnp.sin(ang)

    const = lambda *shape: pl.BlockSpec(shape, lambda i, j: (0,) * len(shape))
    in_specs = [pl.BlockSpec((1, tl, d), lambda i, j: (i, j, 0)),
                pl.BlockSpec((tl, half), lambda i, j: (j, 0)),
                pl.BlockSpec((tl, half), lambda i, j: (j, 0))]
    args = [x, cos, sin]
    if has_state:
        in_specs += [pl.BlockSpec((1, 2, d), lambda i, j: (i, 0, 0)),
                     pl.BlockSpec((1, RET_HEADS, dk, dk), lambda i, j: (i, 0, 0, 0))]
        args += [conv_state, ret_state]
    in_specs += [const(1, d), const(d, win_bf.shape[1]), const(3, d), const(d, d)]
    args += [g1.reshape(1, d), win_bf, cw, wout_bf]
    return pl.pallas_call(
        functools.partial(_mixer_kernel, tl=tl, d=d, has_state=has_state),
        grid=(b, l // tl),
        in_specs=in_specs,
        out_specs=[pl.BlockSpec((1, tl, d), lambda i, j: (i, j, 0)),
                   pl.BlockSpec((1, 2, d), lambda i, j: (i, 0, 0)),
                   pl.BlockSpec((1, RET_HEADS, dk, dk), lambda i, j: (i, 0, 0, 0))],
        out_shape=[jax.ShapeDtypeStruct((b, l, d), F32),
                   jax.ShapeDtypeStruct((b, 2, d), F32),
                   jax.ShapeDtypeStruct((b, RET_HEADS, dk, dk), F32)],
        compiler_params=pltpu.CompilerParams(
            dimension_semantics=("arbitrary", "arbitrary"), vmem_limit_bytes=VMEM_LIMIT_BYTES),
        name="mixer",
    )(*args)


def _oddeven_merge_sort_pairs(n):
    pairs = []
    p = 1
    while p < n:
        k = p
        while k >= 1:
            for j in range(k % p, n - k, 2 * k):
                for i in range(min(k, n - j - k)):
                    if (i + j) // (2 * p) == (i + j + k) // (2 * p):
                        pairs.append((i + j, i + j + k))
            k //= 2
        p *= 2
    return pairs


SORT16_PAIRS = tuple(_oddeven_merge_sort_pairs(PEER_TOPK))
SUBLANES = 8


def _bitonic_to_sorted(z):
    z = list(z)
    d = PEER_TOPK // 2
    while d >= 1:
        for r in range(PEER_TOPK):
            if r % (2 * d) < d:
                hi = jnp.maximum(z[r], z[r + d])
                lo = jnp.minimum(z[r], z[r + d])
                z[r], z[r + d] = hi, lo
        d //= 2
    return z


def _merge_sublanes(x):
    shift = SUBLANES // 2
    while shift >= 1:
        other = [pltpu.roll(x[PEER_TOPK - 1 - r], shift, 0) for r in range(PEER_TOPK)]
        x = _bitonic_to_sorted([jnp.maximum(x[r], other[r]) for r in range(PEER_TOPK)])
        shift //= 2
    return x


def _top16_sorted(rows):
    x = list(rows)
    for a, b in SORT16_PAIRS:
        hi = jnp.maximum(x[a], x[b])
        lo = jnp.minimum(x[a], x[b])
        x[a], x[b] = hi, lo
    return _merge_sublanes(x)


def _peer_sel_kernel(x1_ref, g2_ref, wqt_ref, sk_ref,
                     h2t_ref, rank1_ref, p1_ref, n0_ref, p0_ref, qt_ref, *, n_keys, d_half):
    assert n_keys == PEER_TOPK * SUBLANES
    h2 = _rmsnorm(x1_ref[...], g2_ref[...])
    h2t = h2.T.astype(BF16)
    h2t_ref[...] = h2t
    qt_ref[...] = _dot(wqt_ref[...], h2t)
    sk0 = sk_ref[0]
    sk1 = sk_ref[1]
    tt = h2t.shape[1]
    sub = lax.broadcasted_iota(jnp.int32, (SUBLANES, tt), 0)

    def head(hd, _):
        base = pl.multiple_of(hd * (2 * d_half), 2 * d_half)
        s0 = _dot(sk0, qt_ref[pl.ds(base, d_half), :].astype(BF16))
        s1 = _dot(sk1, qt_ref[pl.ds(base + d_half, d_half), :].astype(BF16))
        rows0 = [s0[r * SUBLANES:(r + 1) * SUBLANES] for r in range(PEER_TOPK)]
        rows1 = [s1[r * SUBLANES:(r + 1) * SUBLANES] for r in range(PEER_TOPK)]
        a0 = _top16_sorted(rows0)
        a1 = _top16_sorted(rows1)
        a0_col = a0[SUBLANES - 1]
        for r0 in range(SUBLANES - 2, -1, -1):
            a0_col = jnp.where(sub == r0, a0[r0], a0_col)
        cand = [a0_col + a1[r1] for r1 in range(PEER_TOPK)]
        top = _merge_sublanes(cand)
        tail = [a0[SUBLANES + k] + a1[0] for k in range(SUBLANES)]
        top = _bitonic_to_sorted(
            top[:SUBLANES] + [jnp.maximum(top[SUBLANES + k], tail[SUBLANES - 1 - k])
                              for k in range(SUBLANES)])
        tau = top[PEER_TOPK - 1]
        z = jnp.ones_like(tau)
        for k in range(1, PEER_TOPK):
            z = z + jnp.exp(top[k] - top[0])
        cnt = jnp.zeros_like(tau)
        for r1 in range(PEER_TOPK):
            cnt = jnp.where(cand[r1] >= tau, float(r1 + 1), cnt)
        n_by_rank = [jnp.broadcast_to(cnt[q:q + 1], cnt.shape) for q in range(SUBLANES)]
        n_by_rank += [jnp.where(tail[k] >= tau, 1.0, 0.0) for k in range(SUBLANES)]
        n0_rows = []
        rank1_rows = []
        for r in range(PEER_TOPK):
            n0 = jnp.zeros_like(tau)
            rank1 = jnp.full(tau.shape, NO_RANK, F32)
            for q in range(PEER_TOPK - 1, -1, -1):
                n0 = jnp.where(rows0[r] == a0[q], n_by_rank[q], n0)
                rank1 = jnp.where(rows1[r] >= a1[q], float(q), rank1)
            n0_rows.append(n0)
            rank1_rows.append(rank1)
        n0_ref[hd] = jnp.concatenate(n0_rows, axis=0)
        p0_ref[hd] = jnp.exp(s0 - jnp.concatenate([a0[0]] * PEER_TOPK, axis=0))
        rank1_ref[hd] = jnp.concatenate(rank1_rows, axis=0).astype(BF16)
        p1_ref[hd] = (jnp.exp(s1 - jnp.concatenate([a1[0]] * PEER_TOPK, axis=0))
                      / jnp.concatenate([z] * PEER_TOPK, axis=0)).astype(BF16)
        return 0

    lax.fori_loop(0, PEER_HEADS, head, 0)


def _peer_sel(x1, g2, wqt_bf, sk_bf, *, tt):
    t, d = x1.shape
    _, n_keys, d_half = sk_bf.shape
    dq = wqt_bf.shape[0]
    tab = lambda dt: jax.ShapeDtypeStruct((PEER_HEADS, n_keys, t), dt)
    tab_spec = pl.BlockSpec((PEER_HEADS, n_keys, tt), lambda i: (0, 0, i))
    return pl.pallas_call(
        functools.partial(_peer_sel_kernel, n_keys=n_keys, d_half=d_half),
        grid=(t // tt,),
        in_specs=[pl.BlockSpec((tt, d), lambda i: (i, 0)),
                  pl.BlockSpec((1, d), lambda i: (0, 0)),
                  pl.BlockSpec((dq, d), lambda i: (0, 0)),
                  pl.BlockSpec((2, n_keys, d_half), lambda i: (0, 0, 0))],
        out_specs=[pl.BlockSpec((d, tt), lambda i: (0, i)), tab_spec, tab_spec, tab_spec, tab_spec],
        out_shape=[jax.ShapeDtypeStruct((d, t), BF16), tab(BF16), tab(BF16), tab(F32), tab(F32)],
        scratch_shapes=[pltpu.VMEM((dq, tt), F32)],
        compiler_params=pltpu.CompilerParams(
            dimension_semantics=("arbitrary",), vmem_limit_bytes=VMEM_LIMIT_BYTES),
        name="peer_sel",
    )(x1, g2.reshape(1, d), wqt_bf, sk_bf)


def _peer_main_kernel(h2t_ref, rank1_ref, p1_ref, n0_ref, p0_ref, u_ref, vt_ref, x1_ref, gf_ref,
                      y_ref, acc_ref, a_ref, g_ref, *, n_keys, te, n_eblk, final_norm):
    s = pl.program_id(0)
    n_pairs = pl.num_programs(0) - 2
    tt = h2t_ref.shape[1]
    slot0 = s % 2
    slot1 = (s + 1) % 2

    @pl.when(s == 0)
    def _():
        a_ref[...] = jnp.zeros_like(a_ref)
        g_ref[...] = jnp.zeros_like(g_ref)

    pair3 = jnp.clip(s - 2, 0, n_pairs - 1)
    e3 = pair3 % n_eblk

    @pl.when(e3 == 0)
    def _():
        acc_ref[...] = jnp.zeros_like(acc_ref)

    a_ref[slot0] = _dot(u_ref[...], h2t_ref[...]).astype(BF16)

    e2 = jnp.clip(s - 1, 0, n_pairs - 1) % n_eblk
    for ii in range(te // n_keys):
        i = e2 * (te // n_keys) + ii
        rows = slice(ii * n_keys, (ii + 1) * n_keys)
        w = jnp.zeros((n_keys, tt), BF16)
        for hd in range(PEER_HEADS):
            n_row = jnp.broadcast_to(n0_ref[hd, pl.ds(i, 1), :], (16, tt)).astype(BF16)
            p_row = jnp.broadcast_to(p0_ref[hd, pl.ds(i, 1), :], (16, tt)).astype(BF16)
            n_b = jnp.broadcast_to(n_row[None], (n_keys // 16, 16, tt)).reshape(n_keys, tt)
            p_b = jnp.broadcast_to(p_row[None], (n_keys // 16, 16, tt)).reshape(n_keys, tt)
            w = w + jnp.where(rank1_ref[hd] < n_b, p1_ref[hd], jnp.zeros((), BF16)) * p_b
        a = a_ref[slot1, rows, :]
        act = (0.5 * a) * (1.0 + lax.erf(a * np.float32(2.0 ** -0.5).astype(BF16)))
        g_ref[slot1, rows, :] = w * act

    acc_ref[...] += _dot(vt_ref[...], g_ref[slot0])

    @pl.when(jnp.logical_and(e3 == n_eblk - 1, s >= 2))
    def _():
        out = x1_ref[...] + acc_ref[...].T
        if final_norm:
            out = _rmsnorm(out, gf_ref[...])
        y_ref[...] = out


def _peer_main(x1, h2t, rank1, p1, n0, p0, u_bf, vt_bf, gf, *, tt, te, final_norm):
    t, d = x1.shape
    n_exp = u_bf.shape[0]
    n_keys = rank1.shape[1]
    n_eblk = n_exp // te
    n_pairs = (t // tt) * n_eblk

    def pair(s, lag):
        p = jnp.clip(s - lag, 0, n_pairs - 1)
        return p // n_eblk, p % n_eblk

    tab_spec = pl.BlockSpec((PEER_HEADS, n_keys, tt), lambda s: (0, 0, pair(s, 1)[0]))
    return pl.pallas_call(
        functools.partial(_peer_main_kernel, n_keys=n_keys, te=te, n_eblk=n_eblk,
                          final_norm=final_norm),
        grid=(n_pairs + 2,),
        in_specs=[pl.BlockSpec((d, tt), lambda s: (0, pair(s, 0)[0])),
                  tab_spec, tab_spec, tab_spec, tab_spec,
                  pl.BlockSpec((te, d), lambda s: (pair(s, 0)[1], 0)),
                  pl.BlockSpec((d, te), lambda s: (0, pair(s, 2)[1])),
                  pl.BlockSpec((tt, d), lambda s: (pair(s, 2)[0], 0)),
                  pl.BlockSpec((1, d), lambda s: (0, 0))],
        out_specs=pl.BlockSpec((tt, d), lambda s: (pair(s, 2)[0], 0)),
        out_shape=jax.ShapeDtypeStruct((t, d), F32),
        scratch_shapes=[pltpu.VMEM((d, tt), F32), pltpu.VMEM((2, te, tt), BF16),
                        pltpu.VMEM((2, te, tt), BF16)],
        compiler_params=pltpu.CompilerParams(
            dimension_semantics=("arbitrary",), vmem_limit_bytes=VMEM_LIMIT_BYTES),
        name="peer_main",
    )(h2t, rank1, p1, n0, p0, u_bf, vt_bf, x1, gf.reshape(1, d))


def _pick_tile(n, cap):
    tile = min(n, cap)
    assert n % tile == 0, (n, tile)
    return tile


def kernel(x_prompt, x_sample, state_conv, state_ret, norm1_g, w_in, conv_w, w_out, norm2_g,
           peer_wq, peer_subkeys, peer_u, peer_v, final_g):
    depth = w_in.shape[0]
    d = x_prompt.shape[-1]
    assert peer_wq.shape[-1] == PEER_HEADS * 2 * peer_subkeys.shape[-1]
    assert peer_u.shape[1] == peer_subkeys.shape[2] ** 2
    streams = [(x_prompt, None, None, 0), (x_sample, state_conv, state_ret, PAST_LEN)]
    xs = [s[0] for s in streams]
    conv_out = [[] for _ in streams]
    ret_out = [[] for _ in streams]
    for layer in range(depth):
        win_bf = w_in[layer].astype(BF16)
        wout_bf = w_out[layer].astype(BF16)
        wqt_bf = peer_wq[layer].T.astype(BF16)
        sk_bf = peer_subkeys[layer].astype(BF16)
        u_bf = peer_u[layer].astype(BF16)
        vt_bf = peer_v[layer].T.astype(BF16)
        for si, (_, cst, rst, pos0) in enumerate(streams):
            x = xs[si]
            b, l, _ = x.shape
            x1, nconv, nret = _mixer(
                x, None if cst is None else cst[layer], None if rst is None else rst[layer], pos0,
                norm1_g[layer], win_bf, conv_w[layer], wout_bf, tl=_pick_tile(l, 256))
            conv_out[si].append(nconv)
            ret_out[si].append(nret)
            x1 = x1.reshape(b * l, d)
            h2t, rank1, p1, n0, p0 = _peer_sel(x1, norm2_g[layer], wqt_bf, sk_bf,
                                               tt=_pick_tile(b * l, 256))
            y = _peer_main(x1, h2t, rank1, p1, n0, p0, u_bf, vt_bf, final_g,
                           tt=_pick_tile(b * l, 512), te=2048, final_norm=layer == depth - 1)
            xs[si] = y.reshape(b, l, d)
    return (xs[0], xs[1], jnp.stack(conv_out[0]), jnp.stack(ret_out[0]),
            jnp.stack(conv_out[1]), jnp.stack(ret_out[1]))
```

```python
import functools

import numpy as np
import jax
import jax.numpy as jnp
from jax import lax
from jax.experimental import pallas as pl
from jax.experimental.pallas import tpu as pltpu

F32 = jnp.float32
BF16 = jnp.bfloat16

EPS = 1e-6
ROPE_BASE = 10000.0
PAST_LEN = 1024
RET_HEADS = 4
PEER_HEADS = 8
PEER_TOPK = 16
LOG_DECAY = tuple(float(np.log1p(-np.exp2(np.float32(-5.0 - h)))) for h in range(RET_HEADS))
NEG_INF = float("-inf")
NO_RANK = 127.0

VMEM_LIMIT_BYTES = 56 * 1024 * 1024
LANES = 128


def _rmsnorm(x, g):
    return x * lax.rsqrt(jnp.mean(x * x, axis=-1, keepdims=True) + EPS) * g


def _dot(a, b):
    return jnp.dot(a, b, preferred_element_type=F32)


def _mixer_kernel(*refs, tl, d, has_state):
    if has_state:
        (x_ref, cos_ref, sin_ref, cst_ref, rst_ref, g1_ref, win_ref, cw_ref, wout_ref,
         x1_ref, nconv_ref, nret_ref) = refs
    else:
        (x_ref, cos_ref, sin_ref, g1_ref, win_ref, cw_ref, wout_ref,
         x1_ref, nconv_ref, nret_ref) = refs
    dk = d // RET_HEADS
    half = dk // 2

    @pl.when(pl.program_id(1) == 0)
    def _():
        if has_state:
            nconv_ref[...] = cst_ref[...]
            nret_ref[...] = rst_ref[...]
        else:
            nconv_ref[...] = jnp.zeros_like(nconv_ref)
            nret_ref[...] = jnp.zeros_like(nret_ref)

    x = x_ref[0]
    h = _rmsnorm(x, g1_ref[...]).astype(BF16)

    def proj(g):
        return _dot(h, win_ref[:, g * d:(g + 1) * d])

    u = proj(1) * proj(2)
    carry = nconv_ref[0]
    row = lax.broadcasted_iota(jnp.int32, (tl, d), 0)
    u1 = jnp.where(row == 0, carry[1:2], pltpu.roll(u, 1, 0))
    u2 = jnp.where(row == 0, carry[0:1], jnp.where(row == 1, carry[1:2], pltpu.roll(u, 2, 0)))
    cw = cw_ref[...]
    y_conv = proj(0) * (u2 * cw[0:1] + u1 * cw[1:2] + u * cw[2:3])
    nconv_ref[0] = u[tl - 2:tl, :]
    gate_a = jax.nn.sigmoid(proj(7))
    m_conv = gate_a * y_conv

    cos = cos_ref[...]
    sin = sin_ref[...]
    q = proj(3)
    k = proj(4)
    v = proj(5)
    n_idx = lax.broadcasted_iota(jnp.int32, (tl, tl), 0)
    m_idx = lax.broadcasted_iota(jnp.int32, (tl, tl), 1)
    diff = (n_idx - m_idx).astype(F32)
    rowf = lax.broadcasted_iota(jnp.int32, (tl, dk), 0).astype(F32)

    def rope(t):
        t1 = t[:, :half]
        t2 = t[:, half:]
        return jnp.concatenate([t1 * cos - t2 * sin, t1 * sin + t2 * cos], axis=-1)

    outs = []
    for hh in range(RET_HEADS):
        lg = LOG_DECAY[hh]
        sl = slice(hh * dk, (hh + 1) * dk)
        qr = rope(q[:, sl])
        kr = rope(k[:, sl]) * (dk ** -0.5)
        vb = v[:, sl].astype(BF16)
        qb = qr.astype(BF16)
        decay = jnp.where(diff >= 0, jnp.exp(lg * jnp.maximum(diff, 0.0)), 0.0)
        att = lax.dot_general(qb, kr.astype(BF16), (((1,), (1,)), ((), ())),
                              preferred_element_type=F32) * decay
        s_old = nret_ref[0, hh]
        o = _dot(att.astype(BF16), vb) + _dot(qb, s_old.astype(BF16)) * jnp.exp(lg * (rowf + 1.0))
        kw = (kr * jnp.exp(lg * (tl - 1.0 - rowf))).astype(BF16)
        nret_ref[0, hh] = float(np.exp(lg * tl)) * s_old + lax.dot_general(
            kw, vb, (((0,), (0,)), ((), ())), preferred_element_type=F32)
        outs.append(o * lax.rsqrt(jnp.mean(o * o, axis=-1, keepdims=True) + EPS))
    o_all = jnp.concatenate(outs, axis=-1)
    rg = proj(6)
    y_ret = rg * jax.nn.sigmoid(rg) * o_all
    m = m_conv + jax.nn.sigmoid(proj(8)) * y_ret
    x1_ref[0] = x + _dot(m.astype(BF16), wout_ref[...])


def _mixer(x, conv_state, ret_state, pos0, g1, win_bf, cw, wout_bf, *, tl):
    b, l, d = x.shape
    dk = d // RET_HEADS
    half = dk // 2
    has_state = conv_state is not None
    inv = ROPE_BASE ** (-jnp.arange(half, dtype=F32) / half)
    ang = (pos0 + jnp.arange(l)).astype(F32)[:, None] * inv[None, :]
    cos, sin = jnp.cos(ang), jnp.sin(ang)

    const = lambda *shape: pl.BlockSpec(shape, lambda i, j: (0,) * len(shape))
    in_specs = [pl.BlockSpec((1, tl, d), lambda i, j: (i, j, 0)),
                pl.BlockSpec((tl, half), lambda i, j: (j, 0)),
                pl.BlockSpec((tl, half), lambda i, j: (j, 0))]
    args = [x, cos, sin]
    if has_state:
        in_specs += [pl.BlockSpec((1, 2, d), lambda i, j: (i, 0, 0)),
                     pl.BlockSpec((1, RET_HEADS, dk, dk), lambda i, j: (i, 0, 0, 0))]
        args += [conv_state, ret_state]
    in_specs += [const(1, d), const(d, win_bf.shape[1]), const(3, d), const(d, d)]
    args += [g1.reshape(1, d), win_bf, cw, wout_bf]
    return pl.pallas_call(
        functools.partial(_mixer_kernel, tl=tl, d=d, has_state=has_state),
        grid=(b, l // tl),
        in_specs=in_specs,
        out_specs=[pl.BlockSpec((1, tl, d), lambda i, j: (i, j, 0)),
                   pl.BlockSpec((1, 2, d), lambda i, j: (i, 0, 0)),
                   pl.BlockSpec((1, RET_HEADS, dk, dk), lambda i, j: (i, 0, 0, 0))],
        out_shape=[jax.ShapeDtypeStruct((b, l, d), F32),
                   jax.ShapeDtypeStruct((b, 2, d), F32),
                   jax.ShapeDtypeStruct((b, RET_HEADS, dk, dk), F32)],
        compiler_params=pltpu.CompilerParams(
            dimension_semantics=("arbitrary", "arbitrary"), vmem_limit_bytes=VMEM_LIMIT_BYTES),
        name="mixer",
    )(*args)


def _oddeven_merge_sort_pairs(n):
    pairs = []
    p = 1
    while p < n:
        k = p
        while k >= 1:
            for j in range(k % p, n - k, 2 * k):
                for i in range(min(k, n - j - k)):
                    if (i + j) // (2 * p) == (i + j + k) // (2 * p):
                        pairs.append((i + j, i + j + k))
            k //= 2
        p *= 2
    return pairs


SORT16_PAIRS = tuple(_oddeven_merge_sort_pairs(PEER_TOPK))
SUBLANES = 8


def _bitonic_to_sorted(z):
    z = list(z)
    d = PEER_TOPK // 2
    while d >= 1:
        for r in range(PEER_TOPK):
            if r % (2 * d) < d:
                hi = jnp.maximum(z[r], z[r + d])
                lo = jnp.minimum(z[r], z[r + d])
                z[r], z[r + d] = hi, lo
        d //= 2
    return z


def _merge_sublanes(x):
    shift = SUBLANES // 2
    while shift >= 1:
        other = [pltpu.roll(x[PEER_TOPK - 1 - r], shift, 0) for r in range(PEER_TOPK)]
        x = _bitonic_to_sorted([jnp.maximum(x[r], other[r]) for r in range(PEER_TOPK)])
        shift //= 2
    return x


def _top16_sorted(rows):
    x = list(rows)
    for a, b in SORT16_PAIRS:
        hi = jnp.maximum(x[a], x[b])
        lo = jnp.minimum(x[a], x[b])
        x[a], x[b] = hi, lo
    return _merge_sublanes(x)


def _peer_sel_kernel(x1_ref, g2_ref, wqt_ref, sk_ref,
                     h2t_ref, rank1_ref, p1_ref, n0_ref, p0_ref, qt_ref, *, n_keys, d_half):
    assert n_keys == PEER_TOPK * SUBLANES
    h2 = _rmsnorm(x1_ref[...], g2_ref[...])
    h2t = h2.T.astype(BF16)
    h2t_ref[...] = h2t
    qt_ref[...] = _dot(wqt_ref[...], h2t)
    sk0 = sk_ref[0]
    sk1 = sk_ref[1]
    tt = h2t.shape[1]
    sub = lax.broadcasted_iota(jnp.int32, (SUBLANES, tt), 0)

    def head(hd, _):
        base = pl.multiple_of(hd * (2 * d_half), 2 * d_half)
        s0 = _dot(sk0, qt_ref[pl.ds(base, d_half), :].astype(BF16))
        s1 = _dot(sk1, qt_ref[pl.ds(base + d_half, d_half), :].astype(BF16))
        rows0 = [s0[r * SUBLANES:(r + 1) * SUBLANES] for r in range(PEER_TOPK)]
        rows1 = [s1[r * SUBLANES:(r + 1) * SUBLANES] for r in range(PEER_TOPK)]
        a0 = _top16_sorted(rows0)
        a1 = _top16_sorted(rows1)
        a0_col = a0[SUBLANES - 1]
        for r0 in range(SUBLANES - 2, -1, -1):
            a0_col = jnp.where(sub == r0, a0[r0], a0_col)
        cand = [a0_col + a1[r1] for r1 in range(PEER_TOPK)]
        top = _merge_sublanes(cand)
        tail = [a0[SUBLANES + k] + a1[0] for k in range(SUBLANES)]
        top = _bitonic_to_sorted(
            top[:SUBLANES] + [jnp.maximum(top[SUBLANES + k], tail[SUBLANES - 1 - k])
                              for k in range(SUBLANES)])
        tau = top[PEER_TOPK - 1]
        z = jnp.ones_like(tau)
        for k in range(1, PEER_TOPK):
            z = z + jnp.exp(top[k] - top[0])
        cnt = jnp.zeros_like(tau)
        for r1 in range(PEER_TOPK):
            cnt = jnp.where(cand[r1] >= tau, float(r1 + 1), cnt)
        n_by_rank = [jnp.broadcast_to(cnt[q:q + 1], cnt.shape) for q in range(SUBLANES)]
        n0_rows = []
        rank1_rows = []
        for r in range(PEER_TOPK):
            n0 = jnp.where(rows0[r] >= a0[PEER_TOPK - 1],
                           jnp.where(rows0[r] + a1[0] >= tau, 1.0, 0.0), 0.0)
            rank1 = jnp.full(tau.shape, NO_RANK, F32)
            for q in range(PEER_TOPK - 1, -1, -1):
                if q < SUBLANES:
                    n0 = jnp.where(rows0[r] == a0[q], n_by_rank[q], n0)
                rank1 = jnp.where(rows1[r] >= a1[q], float(q), rank1)
            n0_rows.append(n0)
            rank1_rows.append(rank1)
        n0_ref[hd] = jnp.concatenate(n0_rows, axis=0)
        p0_ref[hd] = jnp.exp(s0 - jnp.concatenate([a0[0]] * PEER_TOPK, axis=0))
        rank1_ref[hd] = jnp.concatenate(rank1_rows, axis=0).astype(BF16)
        p1_ref[hd] = (jnp.exp(s1 - jnp.concatenate([a1[0]] * PEER_TOPK, axis=0))
                      / jnp.concatenate([z] * PEER_TOPK, axis=0)).astype(BF16)
        return 0

    lax.fori_loop(0, PEER_HEADS, head, 0)


def _peer_sel(x1, g2, wqt_bf, sk_bf, *, tt):
    t, d = x1.shape
    _, n_keys, d_half = sk_bf.shape
    dq = wqt_bf.shape[0]
    tab = lambda dt: jax.ShapeDtypeStruct((PEER_HEADS, n_keys, t), dt)
    tab_spec = pl.BlockSpec((PEER_HEADS, n_keys, tt), lambda i: (0, 0, i))
    return pl.pallas_call(
        functools.partial(_peer_sel_kernel, n_keys=n_keys, d_half=d_half),
        grid=(t // tt,),
        in_specs=[pl.BlockSpec((tt, d), lambda i: (i, 0)),
                  pl.BlockSpec((1, d), lambda i: (0, 0)),
                  pl.BlockSpec((dq, d), lambda i: (0, 0)),
                  pl.BlockSpec((2, n_keys, d_half), lambda i: (0, 0, 0))],
        out_specs=[pl.BlockSpec((d, tt), lambda i: (0, i)), tab_spec, tab_spec, tab_spec, tab_spec],
        out_shape=[jax.ShapeDtypeStruct((d, t), BF16), tab(BF16), tab(BF16), tab(F32), tab(F32)],
        scratch_shapes=[pltpu.VMEM((dq, tt), F32)],
        compiler_params=pltpu.CompilerParams(
            dimension_semantics=("arbitrary",), vmem_limit_bytes=VMEM_LIMIT_BYTES),
        name="peer_sel",
    )(x1, g2.reshape(1, d), wqt_bf, sk_bf)


def _peer_main_kernel(h2t_ref, rank1_ref, p1_ref, n0_ref, p0_ref, u_ref, vt_ref, x1_ref, gf_ref,
                      y_ref, acc_ref, a_ref, g_ref, *, n_keys, te, n_eblk, final_norm):
    s = pl.program_id(0)
    n_pairs = pl.num_programs(0) - 2
    tt = h2t_ref.shape[1]
    slot0 = s % 2
    slot1 = (s + 1) % 2

    @pl.when(s == 0)
    def _():
        a_ref[...] = jnp.zeros_like(a_ref)
        g_ref[...] = jnp.zeros_like(g_ref)

    pair3 = jnp.clip(s - 2, 0, n_pairs - 1)
    e3 = pair3 % n_eblk

    @pl.when(e3 == 0)
    def _():
        acc_ref[...] = jnp.zeros_like(acc_ref)

    a_ref[slot0] = _dot(pltpu.bitcast(u_ref[...], BF16), h2t_ref[...]).astype(BF16)

    e2 = jnp.clip(s - 1, 0, n_pairs - 1) % n_eblk
    for ii in range(te // n_keys):
        i = e2 * (te // n_keys) + ii
        rows = slice(ii * n_keys, (ii + 1) * n_keys)
        w = jnp.zeros((n_keys, tt), BF16)
        for hd in range(PEER_HEADS):
            n_row = jnp.broadcast_to(n0_ref[hd, pl.ds(i, 1), :], (16, tt)).astype(BF16)
            p_row = jnp.broadcast_to(p0_ref[hd, pl.ds(i, 1), :], (16, tt)).astype(BF16)
            n_b = jnp.broadcast_to(n_row[None], (n_keys // 16, 16, tt)).reshape(n_keys, tt)
            p_b = jnp.broadcast_to(p_row[None], (n_keys // 16, 16, tt)).reshape(n_keys, tt)
            w = w + jnp.where(rank1_ref[hd] < n_b, p1_ref[hd], jnp.zeros((), BF16)) * p_b
        a = a_ref[slot1, rows, :]
        act = (0.5 * a) * (1.0 + lax.erf(a * np.float32(2.0 ** -0.5).astype(BF16)))
        g_ref[slot1, rows, :] = w * act

    acc_ref[...] += _dot(pltpu.bitcast(vt_ref[...], BF16), g_ref[slot0])

    @pl.when(jnp.logical_and(e3 == n_eblk - 1, s >= 2))
    def _():
        out = x1_ref[...] + acc_ref[...].T
        if final_norm:
            out = _rmsnorm(out, gf_ref[...])
        y_ref[...] = out


def _peer_main(x1, h2t, rank1, p1, n0, p0, u_pk, vt_pk, gf, *, tt, te, final_norm):
    t, d = x1.shape
    n_exp = vt_pk.shape[1]
    n_keys = rank1.shape[1]
    n_eblk = n_exp // te
    n_pairs = (t // tt) * n_eblk

    def pair(s, lag):
        p = jnp.clip(s - lag, 0, n_pairs - 1)
        return p // n_eblk, p % n_eblk

    tab_spec = pl.BlockSpec((PEER_HEADS, n_keys, tt), lambda s: (0, 0, pair(s, 1)[0]))
    return pl.pallas_call(
        functools.partial(_peer_main_kernel, n_keys=n_keys, te=te, n_eblk=n_eblk,
                          final_norm=final_norm),
        grid=(n_pairs + 2,),
        in_specs=[pl.BlockSpec((d, tt), lambda s: (0, pair(s, 0)[0])),
                  tab_spec, tab_spec, tab_spec, tab_spec,
                  pl.BlockSpec((te // 2, d), lambda s: (pair(s, 0)[1], 0)),
                  pl.BlockSpec((d // 2, te), lambda s: (0, pair(s, 2)[1])),
                  pl.BlockSpec((tt, d), lambda s: (pair(s, 2)[0], 0)),
                  pl.BlockSpec((1, d), lambda s: (0, 0))],
        out_specs=pl.BlockSpec((tt, d), lambda s: (pair(s, 2)[0], 0)),
        out_shape=jax.ShapeDtypeStruct((t, d), F32),
        scratch_shapes=[pltpu.VMEM((d, tt), F32), pltpu.VMEM((2, te, tt), BF16),
                        pltpu.VMEM((2, te, tt), BF16)],
        compiler_params=pltpu.CompilerParams(
            dimension_semantics=("arbitrary",), vmem_limit_bytes=VMEM_LIMIT_BYTES),
        name="peer_main",
    )(h2t, rank1, p1, n0, p0, u_pk, vt_pk, x1, gf.reshape(1, d))


def _pack_row_pairs(x):
    bits = lax.bitcast_convert_type(x, jnp.uint16).astype(jnp.uint32)
    return bits[0::2] | (bits[1::2] << 16)


def _pick_tile(n, cap):
    tile = min(n, cap)
    assert n % tile == 0, (n, tile)
    return tile


def kernel(x_prompt, x_sample, state_conv, state_ret, norm1_g, w_in, conv_w, w_out, norm2_g,
           peer_wq, peer_subkeys, peer_u, peer_v, final_g):
    depth = w_in.shape[0]
    d = x_prompt.shape[-1]
    assert peer_wq.shape[-1] == PEER_HEADS * 2 * peer_subkeys.shape[-1]
    assert peer_u.shape[1] == peer_subkeys.shape[2] ** 2
    streams = [(x_prompt, None, None, 0), (x_sample, state_conv, state_ret, PAST_LEN)]
    xs = [s[0] for s in streams]
    conv_out = [[] for _ in streams]
    ret_out = [[] for _ in streams]
    for layer in range(depth):
        win_bf = w_in[layer].astype(BF16)
        wout_bf = w_out[layer].astype(BF16)
        wqt_bf = peer_wq[layer].T.astype(BF16)
        sk_bf = peer_subkeys[layer].astype(BF16)
        u_pk = _pack_row_pairs(peer_u[layer].astype(BF16))
        vt_pk = _pack_row_pairs(peer_v[layer].T.astype(BF16))
        for si, (_, cst, rst, pos0) in enumerate(streams):
            x = xs[si]
            b, l, _ = x.shape
            x1, nconv, nret = _mixer(
                x, None if cst is None else cst[layer], None if rst is None else rst[layer], pos0,
                norm1_g[layer], win_bf, conv_w[layer], wout_bf, tl=_pick_tile(l, 256))
            conv_out[si].append(nconv)
            ret_out[si].append(nret)
            x1 = x1.reshape(b * l, d)
            h2t, rank1, p1, n0, p0 = _peer_sel(x1, norm2_g[layer], wqt_bf, sk_bf,
                                               tt=_pick_tile(b * l, 256))
            y = _peer_main(x1, h2t, rank1, p1, n0, p0, u_pk, vt_pk, final_g,
                           tt=_pick_tile(b * l, 512), te=2048, final_norm=layer == depth - 1)
            xs[si] = y.reshape(b, l, d)
    return (xs[0], xs[1], jnp.stack(conv_out[0]), jnp.stack(ret_out[0]),
            jnp.stack(conv_out[1]), jnp.stack(ret_out[1]))
```

```python
import functools

import numpy as np
import jax
import jax.numpy as jnp
from jax import lax
from jax.experimental import pallas as pl
from jax.experimental.pallas import tpu as pltpu

F32 = jnp.float32
BF16 = jnp.bfloat16

EPS = 1e-6
ROPE_BASE = 10000.0
PAST_LEN = 1024
RET_HEADS = 4
PEER_HEADS = 8
PEER_TOPK = 16
LOG_DECAY = tuple(float(np.log1p(-np.exp2(np.float32(-5.0 - h)))) for h in range(RET_HEADS))
NEG_INF = float("-inf")
NO_RANK = 127.0

VMEM_LIMIT_BYTES = 56 * 1024 * 1024
LANES = 128


def _rmsnorm(x, g):
    return x * lax.rsqrt(jnp.mean(x * x, axis=-1, keepdims=True) + EPS) * g


def _dot(a, b):
    return jnp.dot(a, b, preferred_element_type=F32)


def _mixer_kernel(*refs, tl, d, has_state):
    if has_state:
        (x_ref, cos_ref, sin_ref, cst_ref, rst_ref, g1_ref, win_ref, cw_ref, wout_ref,
         x1_ref, nconv_ref, nret_ref) = refs
    else:
        (x_ref, cos_ref, sin_ref, g1_ref, win_ref, cw_ref, wout_ref,
         x1_ref, nconv_ref, nret_ref) = refs
    dk = d // RET_HEADS
    half = dk // 2

    @pl.when(pl.program_id(1) == 0)
    def _():
        if has_state:
            nconv_ref[...] = cst_ref[...]
            nret_ref[...] = rst_ref[...]
        else:
            nconv_ref[...] = jnp.zeros_like(nconv_ref)
            nret_ref[...] = jnp.zeros_like(nret_ref)

    x = x_ref[0]
    h = _rmsnorm(x, g1_ref[...]).astype(BF16)

    def proj(g):
        return _dot(h, win_ref[:, g * d:(g + 1) * d])

    u = proj(1) * proj(2)
    carry = nconv_ref[0]
    row = lax.broadcasted_iota(jnp.int32, (tl, d), 0)
    u1 = jnp.where(row == 0, carry[1:2], pltpu.roll(u, 1, 0))
    u2 = jnp.where(row == 0, carry[0:1], jnp.where(row == 1, carry[1:2], pltpu.roll(u, 2, 0)))
    cw = cw_ref[...]
    y_conv = proj(0) * (u2 * cw[0:1] + u1 * cw[1:2] + u * cw[2:3])
    nconv_ref[0] = u[tl - 2:tl, :]
    gate_a = jax.nn.sigmoid(proj(7))
    m_conv = gate_a * y_conv

    cos = cos_ref[...]
    sin = sin_ref[...]
    q = proj(3)
    k = proj(4)
    v = proj(5)
    n_idx = lax.broadcasted_iota(jnp.int32, (tl, tl), 0)
    m_idx = lax.broadcasted_iota(jnp.int32, (tl, tl), 1)
    diff = (n_idx - m_idx).astype(F32)
    rowf = lax.broadcasted_iota(jnp.int32, (tl, dk), 0).astype(F32)

    def rope(t):
        t1 = t[:, :half]
        t2 = t[:, half:]
        return jnp.concatenate([t1 * cos - t2 * sin, t1 * sin + t2 * cos], axis=-1)

    outs = []
    for hh in range(RET_HEADS):
        lg = LOG_DECAY[hh]
        sl = slice(hh * dk, (hh + 1) * dk)
        qr = rope(q[:, sl])
        kr = rope(k[:, sl]) * (dk ** -0.5)
        vb = v[:, sl].astype(BF16)
        qb = qr.astype(BF16)
        decay = jnp.where(diff >= 0, jnp.exp(lg * jnp.maximum(diff, 0.0)), 0.0)
        att = lax.dot_general(qb, kr.astype(BF16), (((1,), (1,)), ((), ())),
                              preferred_element_type=F32) * decay
        s_old = nret_ref[0, hh]
        o = _dot(att.astype(BF16), vb) + _dot(qb, s_old.astype(BF16)) * jnp.exp(lg * (rowf + 1.0))
        kw = (kr * jnp.exp(lg * (tl - 1.0 - rowf))).astype(BF16)
        nret_ref[0, hh] = float(np.exp(lg * tl)) * s_old + lax.dot_general(
            kw, vb, (((0,), (0,)), ((), ())), preferred_element_type=F32)
        outs.append(o * lax.rsqrt(jnp.mean(o * o, axis=-1, keepdims=True) + EPS))
    o_all = jnp.concatenate(outs, axis=-1)
    rg = proj(6)
    y_ret = rg * jax.nn.sigmoid(rg) * o_all
    m = m_conv + jax.nn.sigmoid(proj(8)) * y_ret
    x1_ref[0] = x + _dot(m.astype(BF16), wout_ref[...])


def _mixer(x, conv_state, ret_state, pos0, g1, win_bf, cw, wout_bf, *, tl):
    b, l, d = x.shape
    dk = d // RET_HEADS
    half = dk // 2
    has_state = conv_state is not None
    inv = ROPE_BASE ** (-jnp.arange(half, dtype=F32) / half)
    ang = (pos0 + jnp.arange(l)).astype(F32)[:, None] * inv[None, :]
    cos, sin = jnp.cos(ang), jnp.sin(ang)

    const = lambda *shape: pl.BlockSpec(shape, lambda i, j: (0,) * len(shape))
    in_specs = [pl.BlockSpec((1, tl, d), lambda i, j: (i, j, 0)),
                pl.BlockSpec((tl, half), lambda i, j: (j, 0)),
                pl.BlockSpec((tl, half), lambda i, j: (j, 0))]
    args = [x, cos, sin]
    if has_state:
        in_specs += [pl.BlockSpec((1, 2, d), lambda i, j: (i, 0, 0)),
                     pl.BlockSpec((1, RET_HEADS, dk, dk), lambda i, j: (i, 0, 0, 0))]
        args += [conv_state, ret_state]
    in_specs += [const(1, d), const(d, win_bf.shape[1]), const(3, d), const(d, d)]
    args += [g1.reshape(1, d), win_bf, cw, wout_bf]
    return pl.pallas_call(
        functools.partial(_mixer_kernel, tl=tl, d=d, has_state=has_state),
        grid=(b, l // tl),
        in_specs=in_specs,
        out_specs=[pl.BlockSpec((1, tl, d), lambda i, j: (i, j, 0)),
                   pl.BlockSpec((1, 2, d), lambda i, j: (i, 0, 0)),
                   pl.BlockSpec((1, RET_HEADS, dk, dk), lambda i, j: (i, 0, 0, 0))],
        out_shape=[jax.ShapeDtypeStruct((b, l, d), F32),
                   jax.ShapeDtypeStruct((b, 2, d), F32),
                   jax.ShapeDtypeStruct((b, RET_HEADS, dk, dk), F32)],
        compiler_params=pltpu.CompilerParams(
            dimension_semantics=("arbitrary", "arbitrary"), vmem_limit_bytes=VMEM_LIMIT_BYTES),
        name="mixer",
    )(*args)


def _oddeven_merge_sort_pairs(n):
    pairs = []
    p = 1
    while p < n:
        k = p
        while k >= 1:
            for j in range(k % p, n - k, 2 * k):
                for i in range(min(k, n - j - k)):
                    if (i + j) // (2 * p) == (i + j + k) // (2 * p):
                        pairs.append((i + j, i + j + k))
            k //= 2
        p *= 2
    return pairs


SORT16_PAIRS = tuple(_oddeven_merge_sort_pairs(PEER_TOPK))
SUBLANES = 8


def _bitonic_to_sorted(z):
    z = list(z)
    d = PEER_TOPK // 2
    while d >= 1:
        for r in range(PEER_TOPK):
            if r % (2 * d) < d:
                hi = jnp.maximum(z[r], z[r + d])
                lo = jnp.minimum(z[r], z[r + d])
                z[r], z[r + d] = hi, lo
        d //= 2
    return z


def _merge_sublanes(x):
    shift = SUBLANES // 2
    while shift >= 1:
        other = [pltpu.roll(x[PEER_TOPK - 1 - r], shift, 0) for r in range(PEER_TOPK)]
        x = _bitonic_to_sorted([jnp.maximum(x[r], other[r]) for r in range(PEER_TOPK)])
        shift //= 2
    return x


def _top16_sorted(rows):
    x = list(rows)
    for a, b in SORT16_PAIRS:
        hi = jnp.maximum(x[a], x[b])
        lo = jnp.minimum(x[a], x[b])
        x[a], x[b] = hi, lo
    return _merge_sublanes(x)


def _peer_sel_kernel(x1_ref, g2_ref, wqt_ref, sk_ref,
                     h2t_ref, rank1_ref, p1_ref, n0_ref, p0_ref, qt_ref, *, n_keys, d_half):
    assert n_keys == PEER_TOPK * SUBLANES
    h2 = _rmsnorm(x1_ref[...], g2_ref[...])
    h2t = h2.T.astype(BF16)
    h2t_ref[...] = h2t
    qt_ref[...] = _dot(wqt_ref[...], h2t)
    sk0 = sk_ref[0]
    sk1 = sk_ref[1]
    tt = h2t.shape[1]
    sub = lax.broadcasted_iota(jnp.int32, (SUBLANES, tt), 0)

    def head(hd, _):
        base = pl.multiple_of(hd * (2 * d_half), 2 * d_half)
        s0 = _dot(sk0, qt_ref[pl.ds(base, d_half), :].astype(BF16))
        s1 = _dot(sk1, qt_ref[pl.ds(base + d_half, d_half), :].astype(BF16))
        rows0 = [s0[r * SUBLANES:(r + 1) * SUBLANES] for r in range(PEER_TOPK)]
        rows1 = [s1[r * SUBLANES:(r + 1) * SUBLANES] for r in range(PEER_TOPK)]
        a0 = _top16_sorted(rows0)
        a1 = _top16_sorted(rows1)
        a0_col = a0[SUBLANES - 1]
        for r0 in range(SUBLANES - 2, -1, -1):
            a0_col = jnp.where(sub == r0, a0[r0], a0_col)
        cand = [a0_col + a1[r1] for r1 in range(PEER_TOPK)]
        top = _merge_sublanes(cand)
        tail = [a0[SUBLANES + k] + a1[0] for k in range(SUBLANES)]
        top = _bitonic_to_sorted(
            top[:SUBLANES] + [jnp.maximum(top[SUBLANES + k], tail[SUBLANES - 1 - k])
                              for k in range(SUBLANES)])
        tau = top[PEER_TOPK - 1]
        z = jnp.ones_like(tau)
        for k in range(1, PEER_TOPK):
            z = z + jnp.exp(top[k] - top[0])
        cnt = jnp.zeros_like(tau)
        for r1 in range(PEER_TOPK):
            cnt = jnp.where(cand[r1] >= tau, float(r1 + 1), cnt)
        n_by_rank = [jnp.broadcast_to(cnt[q:q + 1], cnt.shape) for q in range(SUBLANES)]
        n0_rows = []
        rank1_rows = []
        for r in range(PEER_TOPK):
            n0 = jnp.where(rows0[r] >= a0[PEER_TOPK - 1],
                           jnp.where(rows0[r] + a1[0] >= tau, 1.0, 0.0), 0.0)
            rank1 = jnp.full(tau.shape, NO_RANK, F32)
            for q in range(PEER_TOPK - 1, -1, -1):
                if q < SUBLANES:
                    n0 = jnp.where(rows0[r] == a0[q], n_by_rank[q], n0)
                rank1 = jnp.where(rows1[r] >= a1[q], float(q), rank1)
            n0_rows.append(n0)
            rank1_rows.append(rank1)
        n0_ref[hd] = jnp.concatenate(n0_rows, axis=0)
        p0_ref[hd] = jnp.exp(s0 - jnp.concatenate([a0[0]] * PEER_TOPK, axis=0))
        rank1_ref[hd] = jnp.concatenate(rank1_rows, axis=0).astype(BF16)
        p1_ref[hd] = (jnp.exp(s1 - jnp.concatenate([a1[0]] * PEER_TOPK, axis=0))
                      / jnp.concatenate([z] * PEER_TOPK, axis=0)).astype(BF16)
        return 0

    lax.fori_loop(0, PEER_HEADS, head, 0)


def _peer_sel(x1, g2, wqt_bf, sk_bf, *, tt):
    t, d = x1.shape
    _, n_keys, d_half = sk_bf.shape
    dq = wqt_bf.shape[0]
    tab = lambda dt: jax.ShapeDtypeStruct((PEER_HEADS, n_keys, t), dt)
    tab_spec = pl.BlockSpec((PEER_HEADS, n_keys, tt), lambda i: (0, 0, i))
    return pl.pallas_call(
        functools.partial(_peer_sel_kernel, n_keys=n_keys, d_half=d_half),
        grid=(t // tt,),
        in_specs=[pl.BlockSpec((tt, d), lambda i: (i, 0)),
                  pl.BlockSpec((1, d), lambda i: (0, 0)),
                  pl.BlockSpec((dq, d), lambda i: (0, 0)),
                  pl.BlockSpec((2, n_keys, d_half), lambda i: (0, 0, 0))],
        out_specs=[pl.BlockSpec((d, tt), lambda i: (0, i)), tab_spec, tab_spec, tab_spec, tab_spec],
        out_shape=[jax.ShapeDtypeStruct((d, t), BF16), tab(BF16), tab(BF16), tab(F32), tab(F32)],
        scratch_shapes=[pltpu.VMEM((dq, tt), F32)],
        compiler_params=pltpu.CompilerParams(
            dimension_semantics=("arbitrary",), vmem_limit_bytes=VMEM_LIMIT_BYTES),
        name="peer_sel",
    )(x1, g2.reshape(1, d), wqt_bf, sk_bf)


def _peer_main_kernel(h2t_ref, rank1_ref, p1_ref, n0_ref, p0_ref, u_ref, vt_ref, x1_ref, gf_ref,
                      y_ref, acc_ref, a_ref, g_ref, *, n_keys, te, n_eblk, final_norm):
    s = pl.program_id(0)
    n_pairs = pl.num_programs(0) - 2
    tt = h2t_ref.shape[1]
    slot0 = s % 2
    slot1 = (s + 1) % 2

    @pl.when(s == 0)
    def _():
        a_ref[...] = jnp.zeros_like(a_ref)
        g_ref[...] = jnp.zeros_like(g_ref)

    pair3 = jnp.clip(s - 2, 0, n_pairs - 1)
    e3 = pair3 % n_eblk

    @pl.when(e3 == 0)
    def _():
        acc_ref[...] = jnp.zeros_like(acc_ref)

    a_ref[slot0] = _dot(pltpu.bitcast(u_ref[...], BF16), h2t_ref[...]).astype(BF16)

    e2 = jnp.clip(s - 1, 0, n_pairs - 1) % n_eblk
    for ii in range(te // n_keys):
        i = e2 * (te // n_keys) + ii
        rows = slice(ii * n_keys, (ii + 1) * n_keys)
        w = jnp.zeros((n_keys, tt), BF16)
        for hd in range(PEER_HEADS):
            n_row = jnp.broadcast_to(n0_ref[hd, pl.ds(i, 1), :], (16, tt)).astype(BF16)
            p_row = jnp.broadcast_to(p0_ref[hd, pl.ds(i, 1), :], (16, tt)).astype(BF16)
            n_b = jnp.broadcast_to(n_row[None], (n_keys // 16, 16, tt)).reshape(n_keys, tt)
            p_b = jnp.broadcast_to(p_row[None], (n_keys // 16, 16, tt)).reshape(n_keys, tt)
            w = w + jnp.where(rank1_ref[hd] < n_b, p1_ref[hd], jnp.zeros((), BF16)) * p_b
        a = a_ref[slot1, rows, :]
        act = (0.5 * a) * (1.0 + lax.erf(a * np.float32(2.0 ** -0.5).astype(BF16)))
        g_ref[slot1, rows, :] = w * act

    acc_ref[...] += _dot(pltpu.bitcast(vt_ref[...], BF16), g_ref[slot0])

    @pl.when(jnp.logical_and(e3 == n_eblk - 1, s >= 2))
    def _():
        out = x1_ref[...] + acc_ref[...].T
        if final_norm:
            out = _rmsnorm(out, gf_ref[...])
        y_ref[...] = out


def _peer_main(x1, h2t, rank1, p1, n0, p0, u_pk, vt_pk, gf, *, tt, te, final_norm):
    t, d = x1.shape
    n_exp = vt_pk.shape[1]
    n_keys = rank1.shape[1]
    n_eblk = n_exp // te
    n_pairs = (t // tt) * n_eblk

    def pair(s, lag):
        p = jnp.clip(s - lag, 0, n_pairs - 1)
        return p // n_eblk, p % n_eblk

    tab_spec = pl.BlockSpec((PEER_HEADS, n_keys, tt), lambda s: (0, 0, pair(s, 1)[0]))
    return pl.pallas_call(
        functools.partial(_peer_main_kernel, n_keys=n_keys, te=te, n_eblk=n_eblk,
                          final_norm=final_norm),
        grid=(n_pairs + 2,),
        in_specs=[pl.BlockSpec((d, tt), lambda s: (0, pair(s, 0)[0])),
                  tab_spec, tab_spec, tab_spec, tab_spec,
                  pl.BlockSpec((te // 2, d), lambda s: (pair(s, 0)[1], 0)),
                  pl.BlockSpec((d // 2, te), lambda s: (0, pair(s, 2)[1])),
                  pl.BlockSpec((tt, d), lambda s: (pair(s, 2)[0], 0)),
                  pl.BlockSpec((1, d), lambda s: (0, 0))],
        out_specs=pl.BlockSpec((tt, d), lambda s: (pair(s, 2)[0], 0)),
        out_shape=jax.ShapeDtypeStruct((t, d), F32),
        scratch_shapes=[pltpu.VMEM((d, tt), F32), pltpu.VMEM((2, te, tt), BF16),
                        pltpu.VMEM((2, te, tt), BF16)],
        compiler_params=pltpu.CompilerParams(
            dimension_semantics=("arbitrary",), vmem_limit_bytes=VMEM_LIMIT_BYTES),
        name="peer_main",
    )(h2t, rank1, p1, n0, p0, u_pk, vt_pk, x1, gf.reshape(1, d))


def _pack_kernel(x_ref, o_ref, *, transpose):
    x = x_ref[...]
    if transpose:
        x = x.T
    o_ref[...] = pltpu.bitcast(x.astype(BF16), jnp.uint32)


def _pack_bf16_pairs(x, *, transpose, rows):
    n, d = x.shape
    if transpose:
        out_shape, out_spec = (d // 2, n), pl.BlockSpec((d // 2, rows), lambda i: (0, i))
    else:
        out_shape, out_spec = (n // 2, d), pl.BlockSpec((rows // 2, d), lambda i: (i, 0))
    return pl.pallas_call(
        functools.partial(_pack_kernel, transpose=transpose),
        grid=(n // rows,),
        in_specs=[pl.BlockSpec((rows, d), lambda i: (i, 0))],
        out_specs=out_spec,
        out_shape=jax.ShapeDtypeStruct(out_shape, jnp.uint32),
        compiler_params=pltpu.CompilerParams(dimension_semantics=("arbitrary",)),
        name="pack_bf16_pairs",
    )(x)


def _pick_tile(n, cap):
    tile = min(n, cap)
    assert n % tile == 0, (n, tile)
    return tile


def kernel(x_prompt, x_sample, state_conv, state_ret, norm1_g, w_in, conv_w, w_out, norm2_g,
           peer_wq, peer_subkeys, peer_u, peer_v, final_g):
    depth = w_in.shape[0]
    d = x_prompt.shape[-1]
    assert peer_wq.shape[-1] == PEER_HEADS * 2 * peer_subkeys.shape[-1]
    assert peer_u.shape[1] == peer_subkeys.shape[2] ** 2
    streams = [(x_prompt, None, None, 0), (x_sample, state_conv, state_ret, PAST_LEN)]
    xs = [s[0] for s in streams]
    conv_out = [[] for _ in streams]
    ret_out = [[] for _ in streams]
    for layer in range(depth):
        win_bf = w_in[layer].astype(BF16)
        wout_bf = w_out[layer].astype(BF16)
        wqt_bf = peer_wq[layer].T.astype(BF16)
        sk_bf = peer_subkeys[layer].astype(BF16)
        u_pk = _pack_bf16_pairs(peer_u[layer], transpose=False, rows=1024)
        vt_pk = _pack_bf16_pairs(peer_v[layer], transpose=True, rows=512)
        for si, (_, cst, rst, pos0) in enumerate(streams):
            x = xs[si]
            b, l, _ = x.shape
            x1, nconv, nret = _mixer(
                x, None if cst is None else cst[layer], None if rst is None else rst[layer], pos0,
                norm1_g[layer], win_bf, conv_w[layer], wout_bf, tl=_pick_tile(l, 256))
            conv_out[si].append(nconv)
            ret_out[si].append(nret)
            x1 = x1.reshape(b * l, d)
            h2t, rank1, p1, n0, p0 = _peer_sel(x1, norm2_g[layer], wqt_bf, sk_bf,
                                               tt=_pick_tile(b * l, 256))
            y = _peer_main(x1, h2t, rank1, p1, n0, p0, u_pk, vt_pk, final_g,
                           tt=_pick_tile(b * l, 512), te=2048, final_norm=layer == depth - 1)
            xs[si] = y.reshape(b, l, d)
    return (xs[0], xs[1], jnp.stack(conv_out[0]), jnp.stack(ret_out[0]),
            jnp.stack(conv_out[1]), jnp.stack(ret_out[1]))
```

```python
import functools

import numpy as np
import jax
import jax.numpy as jnp
from jax import lax
from jax.experimental import pallas as pl
from jax.experimental.pallas import tpu as pltpu

F32 = jnp.float32
BF16 = jnp.bfloat16

EPS = 1e-6
ROPE_BASE = 10000.0
PAST_LEN = 1024
RET_HEADS = 4
PEER_HEADS = 8
PEER_TOPK = 16
LOG_DECAY = tuple(float(np.log1p(-np.exp2(np.float32(-5.0 - h)))) for h in range(RET_HEADS))
NEG_INF = float("-inf")
NO_RANK = 127.0

VMEM_LIMIT_BYTES = 56 * 1024 * 1024
LANES = 128


def _rmsnorm(x, g):
    return x * lax.rsqrt(jnp.mean(x * x, axis=-1, keepdims=True) + EPS) * g


def _dot(a, b):
    return jnp.dot(a, b, preferred_element_type=F32)


def _mixer_kernel(*refs, tl, d, has_state):
    if has_state:
        (x_ref, cos_ref, sin_ref, cst_ref, rst_ref, g1_ref, win_ref, cw_ref, wout_ref,
         x1_ref, nconv_ref, nret_ref) = refs
    else:
        (x_ref, cos_ref, sin_ref, g1_ref, win_ref, cw_ref, wout_ref,
         x1_ref, nconv_ref, nret_ref) = refs
    dk = d // RET_HEADS
    half = dk // 2

    @pl.when(pl.program_id(1) == 0)
    def _():
        if has_state:
            nconv_ref[...] = cst_ref[...]
            nret_ref[...] = rst_ref[...]
        else:
            nconv_ref[...] = jnp.zeros_like(nconv_ref)
            nret_ref[...] = jnp.zeros_like(nret_ref)

    x = x_ref[0]
    h = _rmsnorm(x, g1_ref[...]).astype(BF16)

    def proj(g):
        return _dot(h, win_ref[:, g * d:(g + 1) * d])

    u = proj(1) * proj(2)
    carry = nconv_ref[0]
    row = lax.broadcasted_iota(jnp.int32, (tl, d), 0)
    u1 = jnp.where(row == 0, carry[1:2], pltpu.roll(u, 1, 0))
    u2 = jnp.where(row == 0, carry[0:1], jnp.where(row == 1, carry[1:2], pltpu.roll(u, 2, 0)))
    cw = cw_ref[...]
    y_conv = proj(0) * (u2 * cw[0:1] + u1 * cw[1:2] + u * cw[2:3])
    nconv_ref[0] = u[tl - 2:tl, :]
    gate_a = jax.nn.sigmoid(proj(7))
    m_conv = gate_a * y_conv

    cos = cos_ref[...]
    sin = sin_ref[...]
    q = proj(3)
    k = proj(4)
    v = proj(5)
    n_idx = lax.broadcasted_iota(jnp.int32, (tl, tl), 0)
    m_idx = lax.broadcasted_iota(jnp.int32, (tl, tl), 1)
    diff = (n_idx - m_idx).astype(F32)
    rowf = lax.broadcasted_iota(jnp.int32, (tl, dk), 0).astype(F32)

    def rope(t):
        t1 = t[:, :half]
        t2 = t[:, half:]
        return jnp.concatenate([t1 * cos - t2 * sin, t1 * sin + t2 * cos], axis=-1)

    outs = []
    for hh in range(RET_HEADS):
        lg = LOG_DECAY[hh]
        sl = slice(hh * dk, (hh + 1) * dk)
        qr = rope(q[:, sl])
        kr = rope(k[:, sl]) * (dk ** -0.5)
        vb = v[:, sl].astype(BF16)
        qb = qr.astype(BF16)
        decay = jnp.where(diff >= 0, jnp.exp(lg * jnp.maximum(diff, 0.0)), 0.0)
        att = lax.dot_general(qb, kr.astype(BF16), (((1,), (1,)), ((), ())),
                              preferred_element_type=F32) * decay
        s_old = nret_ref[0, hh]
        o = _dot(att.astype(BF16), vb) + _dot(qb, s_old.astype(BF16)) * jnp.exp(lg * (rowf + 1.0))
        kw = (kr * jnp.exp(lg * (tl - 1.0 - rowf))).astype(BF16)
        nret_ref[0, hh] = float(np.exp(lg * tl)) * s_old + lax.dot_general(
            kw, vb, (((0,), (0,)), ((), ())), preferred_element_type=F32)
        outs.append(o * lax.rsqrt(jnp.mean(o * o, axis=-1, keepdims=True) + EPS))
    o_all = jnp.concatenate(outs, axis=-1)
    rg = proj(6)
    y_ret = rg * jax.nn.sigmoid(rg) * o_all
    m = m_conv + jax.nn.sigmoid(proj(8)) * y_ret
    x1_ref[0] = x + _dot(m.astype(BF16), wout_ref[...])


def _mixer(x, conv_state, ret_state, pos0, g1, win_bf, cw, wout_bf, *, tl):
    b, l, d = x.shape
    dk = d // RET_HEADS
    half = dk // 2
    has_state = conv_state is not None
    inv = ROPE_BASE ** (-jnp.arange(half, dtype=F32) / half)
    ang = (pos0 + jnp.arange(l)).astype(F32)[:, None] * inv[None, :]
    cos, sin = jnp.cos(ang), jnp.sin(ang)

    const = lambda *shape: pl.BlockSpec(shape, lambda i, j: (0,) * len(shape))
    in_specs = [pl.BlockSpec((1, tl, d), lambda i, j: (i, j, 0)),
                pl.BlockSpec((tl, half), lambda i, j: (j, 0)),
                pl.BlockSpec((tl, half), lambda i, j: (j, 0))]
    args = [x, cos, sin]
    if has_state:
        in_specs += [pl.BlockSpec((1, 2, d), lambda i, j: (i, 0, 0)),
                     pl.BlockSpec((1, RET_HEADS, dk, dk), lambda i, j: (i, 0, 0, 0))]
        args += [conv_state, ret_state]
    in_specs += [const(1, d), const(d, win_bf.shape[1]), const(3, d), const(d, d)]
    args += [g1.reshape(1, d), win_bf, cw, wout_bf]
    return pl.pallas_call(
        functools.partial(_mixer_kernel, tl=tl, d=d, has_state=has_state),
        grid=(b, l // tl),
        in_specs=in_specs,
        out_specs=[pl.BlockSpec((1, tl, d), lambda i, j: (i, j, 0)),
                   pl.BlockSpec((1, 2, d), lambda i, j: (i, 0, 0)),
                   pl.BlockSpec((1, RET_HEADS, dk, dk), lambda i, j: (i, 0, 0, 0))],
        out_shape=[jax.ShapeDtypeStruct((b, l, d), F32),
                   jax.ShapeDtypeStruct((b, 2, d), F32),
                   jax.ShapeDtypeStruct((b, RET_HEADS, dk, dk), F32)],
        compiler_params=pltpu.CompilerParams(
            dimension_semantics=("arbitrary", "arbitrary"), vmem_limit_bytes=VMEM_LIMIT_BYTES),
        name="mixer",
    )(*args)


def _oddeven_merge_sort_pairs(n):
    pairs = []
    p = 1
    while p < n:
        k = p
        while k >= 1:
            for j in range(k % p, n - k, 2 * k):
                for i in range(min(k, n - j - k)):
                    if (i + j) // (2 * p) == (i + j + k) // (2 * p):
                        pairs.append((i + j, i + j + k))
            k //= 2
        p *= 2
    return pairs


SORT16_PAIRS = tuple(_oddeven_merge_sort_pairs(PEER_TOPK))
SUBLANES = 8


def _bitonic_to_sorted(z):
    z = list(z)
    d = PEER_TOPK // 2
    while d >= 1:
        for r in range(PEER_TOPK):
            if r % (2 * d) < d:
                hi = jnp.maximum(z[r], z[r + d])
                lo = jnp.minimum(z[r], z[r + d])
                z[r], z[r + d] = hi, lo
        d //= 2
    return z


def _merge_sublanes(x):
    shift = SUBLANES // 2
    while shift >= 1:
        other = [pltpu.roll(x[PEER_TOPK - 1 - r], shift, 0) for r in range(PEER_TOPK)]
        x = _bitonic_to_sorted([jnp.maximum(x[r], other[r]) for r in range(PEER_TOPK)])
        shift //= 2
    return x


def _top16_sorted(rows):
    x = list(rows)
    for a, b in SORT16_PAIRS:
        hi = jnp.maximum(x[a], x[b])
        lo = jnp.minimum(x[a], x[b])
        x[a], x[b] = hi, lo
    return _merge_sublanes(x)


def _peer_sel_kernel(x1_ref, g2_ref, wqt_ref, sk_ref,
                     h2t_ref, rank1_ref, p1_ref, n0_ref, p0_ref, qt_ref, *, n_keys, d_half):
    assert n_keys == PEER_TOPK * SUBLANES
    h2 = _rmsnorm(x1_ref[...], g2_ref[...])
    h2t = h2.T.astype(BF16)
    h2t_ref[...] = h2t
    qt_ref[...] = _dot(wqt_ref[...], h2t)
    sk0 = sk_ref[0]
    sk1 = sk_ref[1]
    tt = h2t.shape[1]
    sub = lax.broadcasted_iota(jnp.int32, (SUBLANES, tt), 0)

    def head(hd, _):
        base = pl.multiple_of(hd * (2 * d_half), 2 * d_half)
        s0 = _dot(sk0, qt_ref[pl.ds(base, d_half), :].astype(BF16))
        s1 = _dot(sk1, qt_ref[pl.ds(base + d_half, d_half), :].astype(BF16))
        rows0 = [s0[r * SUBLANES:(r + 1) * SUBLANES] for r in range(PEER_TOPK)]
        rows1 = [s1[r * SUBLANES:(r + 1) * SUBLANES] for r in range(PEER_TOPK)]
        a0 = _top16_sorted(rows0)
        a1 = _top16_sorted(rows1)
        a0_col = a0[SUBLANES - 1]
        for r0 in range(SUBLANES - 2, -1, -1):
            a0_col = jnp.where(sub == r0, a0[r0], a0_col)
        cand = [a0_col + a1[r1] for r1 in range(PEER_TOPK)]
        top = _merge_sublanes(cand)
        tail = [a0[SUBLANES + k] + a1[0] for k in range(SUBLANES)]
        top = _bitonic_to_sorted(
            top[:SUBLANES] + [jnp.maximum(top[SUBLANES + k], tail[SUBLANES - 1 - k])
                              for k in range(SUBLANES)])
        tau = top[PEER_TOPK - 1]
        z = jnp.ones_like(tau)
        for k in range(1, PEER_TOPK):
            z = z + jnp.exp(top[k] - top[0])
        cnt = jnp.zeros_like(tau)
        for r1 in range(PEER_TOPK):
            cnt = jnp.where(cand[r1] >= tau, float(r1 + 1), cnt)
        n_by_rank = [jnp.broadcast_to(cnt[q:q + 1], cnt.shape) for q in range(SUBLANES)]
        n0_rows = []
        rank1_rows = []
        for r in range(PEER_TOPK):
            n0 = jnp.where(rows0[r] >= a0[PEER_TOPK - 1],
                           jnp.where(rows0[r] + a1[0] >= tau, 1.0, 0.0), 0.0)
            rank1 = jnp.full(tau.shape, NO_RANK, F32)
            for q in range(PEER_TOPK - 1, -1, -1):
                if q < SUBLANES:
                    n0 = jnp.where(rows0[r] == a0[q], n_by_rank[q], n0)
                rank1 = jnp.where(rows1[r] >= a1[q], float(q), rank1)
            n0_rows.append(n0)
            rank1_rows.append(rank1)
        n0_ref[hd] = jnp.concatenate(n0_rows, axis=0)
        p0_ref[hd] = jnp.exp(s0 - jnp.concatenate([a0[0]] * PEER_TOPK, axis=0))
        rank1_ref[hd] = jnp.concatenate(rank1_rows, axis=0).astype(BF16)
        p1_ref[hd] = (jnp.exp(s1 - jnp.concatenate([a1[0]] * PEER_TOPK, axis=0))
                      / jnp.concatenate([z] * PEER_TOPK, axis=0)).astype(BF16)
        return 0

    lax.fori_loop(0, PEER_HEADS, head, 0)


def _peer_sel(x1, g2, wqt_bf, sk_bf, *, tt):
    t, d = x1.shape
    _, n_keys, d_half = sk_bf.shape
    dq = wqt_bf.shape[0]
    tab = lambda dt: jax.ShapeDtypeStruct((PEER_HEADS, n_keys, t), dt)
    tab_spec = pl.BlockSpec((PEER_HEADS, n_keys, tt), lambda i: (0, 0, i))
    return pl.pallas_call(
        functools.partial(_peer_sel_kernel, n_keys=n_keys, d_half=d_half),
        grid=(t // tt,),
        in_specs=[pl.BlockSpec((tt, d), lambda i: (i, 0)),
                  pl.BlockSpec((1, d), lambda i: (0, 0)),
                  pl.BlockSpec((dq, d), lambda i: (0, 0)),
                  pl.BlockSpec((2, n_keys, d_half), lambda i: (0, 0, 0))],
        out_specs=[pl.BlockSpec((d, tt), lambda i: (0, i)), tab_spec, tab_spec, tab_spec, tab_spec],
        out_shape=[jax.ShapeDtypeStruct((d, t), BF16), tab(BF16), tab(BF16), tab(F32), tab(F32)],
        scratch_shapes=[pltpu.VMEM((dq, tt), F32)],
        compiler_params=pltpu.CompilerParams(
            dimension_semantics=("arbitrary",), vmem_limit_bytes=VMEM_LIMIT_BYTES),
        name="peer_sel",
    )(x1, g2.reshape(1, d), wqt_bf, sk_bf)


def _peer_main_kernel(h2t_ref, rank1_ref, p1_ref, n0_ref, p0_ref, u_ref, vt_ref, x1_ref, gf_ref,
                      y_ref, acc_ref, a_ref, g_ref, *, n_keys, te, n_eblk, final_norm):
    s = pl.program_id(0)
    n_pairs = pl.num_programs(0) - 2
    tt = h2t_ref.shape[1]
    slot0 = s % 2
    slot1 = (s + 1) % 2

    @pl.when(s == 0)
    def _():
        a_ref[...] = jnp.zeros_like(a_ref)
        g_ref[...] = jnp.zeros_like(g_ref)

    pair3 = jnp.clip(s - 2, 0, n_pairs - 1)
    e3 = pair3 % n_eblk

    @pl.when(e3 == 0)
    def _():
        acc_ref[...] = jnp.zeros_like(acc_ref)

    acc_ref[...] += _dot(pltpu.bitcast(vt_ref[...], BF16), g_ref[slot0])

    e2 = jnp.clip(s - 1, 0, n_pairs - 1) % n_eblk
    for ii in range(te // n_keys):
        i = e2 * (te // n_keys) + ii
        rows = slice(ii * n_keys, (ii + 1) * n_keys)
        w = jnp.zeros((n_keys, tt), BF16)
        for hd in range(PEER_HEADS):
            n_row = jnp.broadcast_to(n0_ref[hd, pl.ds(i, 1), :], (16, tt)).astype(BF16)
            p_row = jnp.broadcast_to(p0_ref[hd, pl.ds(i, 1), :], (16, tt)).astype(BF16)
            n_b = jnp.broadcast_to(n_row[None], (n_keys // 16, 16, tt)).reshape(n_keys, tt)
            p_b = jnp.broadcast_to(p_row[None], (n_keys // 16, 16, tt)).reshape(n_keys, tt)
            w = w + jnp.where(rank1_ref[hd] < n_b, p1_ref[hd], jnp.zeros((), BF16)) * p_b
        a = a_ref[slot1, rows, :]
        act = (0.5 * a) * (1.0 + lax.erf(a * np.float32(2.0 ** -0.5).astype(BF16)))
        g_ref[slot1, rows, :] = w * act

    a_ref[slot0] = _dot(pltpu.bitcast(u_ref[...], BF16), h2t_ref[...]).astype(BF16)

    @pl.when(jnp.logical_and(e3 == n_eblk - 1, s >= 2))
    def _():
        out = x1_ref[...] + acc_ref[...].T
        if final_norm:
            out = _rmsnorm(out, gf_ref[...])
        y_ref[...] = out


def _peer_main(x1, h2t, rank1, p1, n0, p0, u_pk, vt_pk, gf, *, tt, te, final_norm):
    t, d = x1.shape
    n_exp = vt_pk.shape[1]
    n_keys = rank1.shape[1]
    n_eblk = n_exp // te
    n_pairs = (t // tt) * n_eblk

    def pair(s, lag):
        p = jnp.clip(s - lag, 0, n_pairs - 1)
        return p // n_eblk, p % n_eblk

    tab_spec = pl.BlockSpec((PEER_HEADS, n_keys, tt), lambda s: (0, 0, pair(s, 1)[0]))
    return pl.pallas_call(
        functools.partial(_peer_main_kernel, n_keys=n_keys, te=te, n_eblk=n_eblk,
                          final_norm=final_norm),
        grid=(n_pairs + 2,),
        in_specs=[pl.BlockSpec((d, tt), lambda s: (0, pair(s, 0)[0])),
                  tab_spec, tab_spec, tab_spec, tab_spec,
                  pl.BlockSpec((te // 2, d), lambda s: (pair(s, 0)[1], 0)),
                  pl.BlockSpec((d // 2, te), lambda s: (0, pair(s, 2)[1])),
                  pl.BlockSpec((tt, d), lambda s: (pair(s, 2)[0], 0)),
                  pl.BlockSpec((1, d), lambda s: (0, 0))],
        out_specs=pl.BlockSpec((tt, d), lambda s: (pair(s, 2)[0], 0)),
        out_shape=jax.ShapeDtypeStruct((t, d), F32),
        scratch_shapes=[pltpu.VMEM((d, tt), F32), pltpu.VMEM((2, te, tt), BF16),
                        pltpu.VMEM((2, te, tt), BF16)],
        compiler_params=pltpu.CompilerParams(
            dimension_semantics=("arbitrary",), vmem_limit_bytes=VMEM_LIMIT_BYTES),
        name="peer_main",
    )(h2t, rank1, p1, n0, p0, u_pk, vt_pk, x1, gf.reshape(1, d))


def _pack_kernel(x_ref, o_ref, *, transpose):
    x = x_ref[...]
    if transpose:
        x = x.T
    o_ref[...] = pltpu.bitcast(x.astype(BF16), jnp.uint32)


def _pack_bf16_pairs(x, *, transpose, rows):
    n, d = x.shape
    if transpose:
        out_shape, out_spec = (d // 2, n), pl.BlockSpec((d // 2, rows), lambda i: (0, i))
    else:
        out_shape, out_spec = (n // 2, d), pl.BlockSpec((rows // 2, d), lambda i: (i, 0))
    return pl.pallas_call(
        functools.partial(_pack_kernel, transpose=transpose),
        grid=(n // rows,),
        in_specs=[pl.BlockSpec((rows, d), lambda i: (i, 0))],
        out_specs=out_spec,
        out_shape=jax.ShapeDtypeStruct(out_shape, jnp.uint32),
        compiler_params=pltpu.CompilerParams(dimension_semantics=("arbitrary",)),
        name="pack_bf16_pairs",
    )(x)


def _pick_tile(n, cap):
    tile = min(n, cap)
    assert n % tile == 0, (n, tile)
    return tile


def kernel(x_prompt, x_sample, state_conv, state_ret, norm1_g, w_in, conv_w, w_out, norm2_g,
           peer_wq, peer_subkeys, peer_u, peer_v, final_g):
    depth = w_in.shape[0]
    d = x_prompt.shape[-1]
    assert peer_wq.shape[-1] == PEER_HEADS * 2 * peer_subkeys.shape[-1]
    assert peer_u.shape[1] == peer_subkeys.shape[2] ** 2
    streams = [(x_prompt, None, None, 0), (x_sample, state_conv, state_ret, PAST_LEN)]
    xs = [s[0] for s in streams]
    conv_out = [[] for _ in streams]
    ret_out = [[] for _ in streams]
    for layer in range(depth):
        win_bf = w_in[layer].astype(BF16)
        wout_bf = w_out[layer].astype(BF16)
        wqt_bf = peer_wq[layer].T.astype(BF16)
        sk_bf = peer_subkeys[layer].astype(BF16)
        u_pk = _pack_bf16_pairs(peer_u[layer], transpose=False, rows=1024)
        vt_pk = _pack_bf16_pairs(peer_v[layer], transpose=True, rows=512)
        for si, (_, cst, rst, pos0) in enumerate(streams):
            x = xs[si]
            b, l, _ = x.shape
            x1, nconv, nret = _mixer(
                x, None if cst is None else cst[layer], None if rst is None else rst[layer], pos0,
                norm1_g[layer], win_bf, conv_w[layer], wout_bf, tl=_pick_tile(l, 256))
            conv_out[si].append(nconv)
            ret_out[si].append(nret)
            x1 = x1.reshape(b * l, d)
            h2t, rank1, p1, n0, p0 = _peer_sel(x1, norm2_g[layer], wqt_bf, sk_bf,
                                               tt=_pick_tile(b * l, 256))
            y = _peer_main(x1, h2t, rank1, p1, n0, p0, u_pk, vt_pk, final_g,
                           tt=_pick_tile(b * l, 512), te=2048, final_norm=layer == depth - 1)
            xs[si] = y.reshape(b, l, d)
    return (xs[0], xs[1], jnp.stack(conv_out[0]), jnp.stack(ret_out[0]),
            jnp.stack(conv_out[1]), jnp.stack(ret_out[1]))
```

```python
import functools

import numpy as np
import jax
import jax.numpy as jnp
from jax import lax
from jax.experimental import pallas as pl
from jax.experimental.pallas import tpu as pltpu

F32 = jnp.float32
BF16 = jnp.bfloat16

EPS = 1e-6
ROPE_BASE = 10000.0
PAST_LEN = 1024
RET_HEADS = 4
PEER_HEADS = 8
PEER_TOPK = 16
LOG_DECAY = tuple(float(np.log1p(-np.exp2(np.float32(-5.0 - h)))) for h in range(RET_HEADS))
NEG_INF = float("-inf")
NO_RANK = 127.0

VMEM_LIMIT_BYTES = 56 * 1024 * 1024
LANES = 128


def _rmsnorm(x, g):
    return x * lax.rsqrt(jnp.mean(x * x, axis=-1, keepdims=True) + EPS) * g


def _dot(a, b):
    return jnp.dot(a, b, preferred_element_type=F32)


def _mixer_kernel(*refs, tl, d, has_state):
    if has_state:
        (x_ref, cos_ref, sin_ref, cst_ref, rst_ref, g1_ref, win_ref, cw_ref, wout_ref,
         x1_ref, nconv_ref, nret_ref) = refs
    else:
        (x_ref, cos_ref, sin_ref, g1_ref, win_ref, cw_ref, wout_ref,
         x1_ref, nconv_ref, nret_ref) = refs
    dk = d // RET_HEADS
    half = dk // 2

    @pl.when(pl.program_id(1) == 0)
    def _():
        if has_state:
            nconv_ref[...] = cst_ref[...]
            nret_ref[...] = rst_ref[...]
        else:
            nconv_ref[...] = jnp.zeros_like(nconv_ref)
            nret_ref[...] = jnp.zeros_like(nret_ref)

    x = x_ref[0]
    h = _rmsnorm(x, g1_ref[...]).astype(BF16)

    def proj(g):
        return _dot(h, win_ref[:, g * d:(g + 1) * d])

    u = proj(1) * proj(2)
    carry = nconv_ref[0]
    row = lax.broadcasted_iota(jnp.int32, (tl, d), 0)
    u1 = jnp.where(row == 0, carry[1:2], pltpu.roll(u, 1, 0))
    u2 = jnp.where(row == 0, carry[0:1], jnp.where(row == 1, carry[1:2], pltpu.roll(u, 2, 0)))
    cw = cw_ref[...]
    y_conv = proj(0) * (u2 * cw[0:1] + u1 * cw[1:2] + u * cw[2:3])
    nconv_ref[0] = u[tl - 2:tl, :]
    gate_a = jax.nn.sigmoid(proj(7))
    m_conv = gate_a * y_conv

    cos = cos_ref[...]
    sin = sin_ref[...]
    q = proj(3)
    k = proj(4)
    v = proj(5)
    n_idx = lax.broadcasted_iota(jnp.int32, (tl, tl), 0)
    m_idx = lax.broadcasted_iota(jnp.int32, (tl, tl), 1)
    diff = (n_idx - m_idx).astype(F32)
    rowf = lax.broadcasted_iota(jnp.int32, (tl, dk), 0).astype(F32)

    def rope(t):
        t1 = t[:, :half]
        t2 = t[:, half:]
        return jnp.concatenate([t1 * cos - t2 * sin, t1 * sin + t2 * cos], axis=-1)

    outs = []
    for hh in range(RET_HEADS):
        lg = LOG_DECAY[hh]
        sl = slice(hh * dk, (hh + 1) * dk)
        qr = rope(q[:, sl])
        kr = rope(k[:, sl]) * (dk ** -0.5)
        vb = v[:, sl].astype(BF16)
        qb = qr.astype(BF16)
        decay = jnp.where(diff >= 0, jnp.exp(lg * jnp.maximum(diff, 0.0)), 0.0)
        att = lax.dot_general(qb, kr.astype(BF16), (((1,), (1,)), ((), ())),
                              preferred_element_type=F32) * decay
        s_old = nret_ref[0, hh]
        o = _dot(att.astype(BF16), vb) + _dot(qb, s_old.astype(BF16)) * jnp.exp(lg * (rowf + 1.0))
        kw = (kr * jnp.exp(lg * (tl - 1.0 - rowf))).astype(BF16)
        nret_ref[0, hh] = float(np.exp(lg * tl)) * s_old + lax.dot_general(
            kw, vb, (((0,), (0,)), ((), ())), preferred_element_type=F32)
        outs.append(o * lax.rsqrt(jnp.mean(o * o, axis=-1, keepdims=True) + EPS))
    o_all = jnp.concatenate(outs, axis=-1)
    rg = proj(6)
    y_ret = rg * jax.nn.sigmoid(rg) * o_all
    m = m_conv + jax.nn.sigmoid(proj(8)) * y_ret
    x1_ref[0] = x + _dot(m.astype(BF16), wout_ref[...])


def _mixer(x, conv_state, ret_state, pos0, g1, win_bf, cw, wout_bf, *, tl):
    b, l, d = x.shape
    dk = d // RET_HEADS
    half = dk // 2
    has_state = conv_state is not None
    inv = ROPE_BASE ** (-jnp.arange(half, dtype=F32) / half)
    ang = (pos0 + jnp.arange(l)).astype(F32)[:, None] * inv[None, :]
    cos, sin = jnp.cos(ang), jnp.sin(ang)

    const = lambda *shape: pl.BlockSpec(shape, lambda i, j: (0,) * len(shape))
    in_specs = [pl.BlockSpec((1, tl, d), lambda i, j: (i, j, 0)),
                pl.BlockSpec((tl, half), lambda i, j: (j, 0)),
                pl.BlockSpec((tl, half), lambda i, j: (j, 0))]
    args = [x, cos, sin]
    if has_state:
        in_specs += [pl.BlockSpec((1, 2, d), lambda i, j: (i, 0, 0)),
                     pl.BlockSpec((1, RET_HEADS, dk, dk), lambda i, j: (i, 0, 0, 0))]
        args += [conv_state, ret_state]
    in_specs += [const(1, d), const(d, win_bf.shape[1]), const(3, d), const(d, d)]
    args += [g1.reshape(1, d), win_bf, cw, wout_bf]
    return pl.pallas_call(
        functools.partial(_mixer_kernel, tl=tl, d=d, has_state=has_state),
        grid=(b, l // tl),
        in_specs=in_specs,
        out_specs=[pl.BlockSpec((1, tl, d), lambda i, j: (i, j, 0)),
                   pl.BlockSpec((1, 2, d), lambda i, j: (i, 0, 0)),
                   pl.BlockSpec((1, RET_HEADS, dk, dk), lambda i, j: (i, 0, 0, 0))],
        out_shape=[jax.ShapeDtypeStruct((b, l, d), F32),
                   jax.ShapeDtypeStruct((b, 2, d), F32),
                   jax.ShapeDtypeStruct((b, RET_HEADS, dk, dk), F32)],
        compiler_params=pltpu.CompilerParams(
            dimension_semantics=("arbitrary", "arbitrary"), vmem_limit_bytes=VMEM_LIMIT_BYTES),
        name="mixer",
    )(*args)


def _oddeven_merge_sort_pairs(n):
    pairs = []
    p = 1
    while p < n:
        k = p
        while k >= 1:
            for j in range(k % p, n - k, 2 * k):
                for i in range(min(k, n - j - k)):
                    if (i + j) // (2 * p) == (i + j + k) // (2 * p):
                        pairs.append((i + j, i + j + k))
            k //= 2
        p *= 2
    return pairs


SORT16_PAIRS = tuple(_oddeven_merge_sort_pairs(PEER_TOPK))
SUBLANES = 8


def _bitonic_to_sorted(z):
    z = list(z)
    d = PEER_TOPK // 2
    while d >= 1:
        for r in range(PEER_TOPK):
            if r % (2 * d) < d:
                hi = jnp.maximum(z[r], z[r + d])
                lo = jnp.minimum(z[r], z[r + d])
                z[r], z[r + d] = hi, lo
        d //= 2
    return z


def _merge_sublanes(x):
    shift = SUBLANES // 2
    while shift >= 1:
        other = [pltpu.roll(x[PEER_TOPK - 1 - r], shift, 0) for r in range(PEER_TOPK)]
        x = _bitonic_to_sorted([jnp.maximum(x[r], other[r]) for r in range(PEER_TOPK)])
        shift //= 2
    return x


def _top16_sorted(rows):
    x = list(rows)
    for a, b in SORT16_PAIRS:
        hi = jnp.maximum(x[a], x[b])
        lo = jnp.minimum(x[a], x[b])
        x[a], x[b] = hi, lo
    return _merge_sublanes(x)


def _peer_sel_kernel(x1_ref, g2_ref, wqt_ref, sk_ref,
                     h2t_ref, rank1_ref, p1_ref, n0_ref, p0_ref, qt_ref, *, n_keys, d_half):
    assert n_keys == PEER_TOPK * SUBLANES
    h2 = _rmsnorm(x1_ref[...], g2_ref[...])
    h2t = h2.T.astype(BF16)
    h2t_ref[...] = h2t
    qt_ref[...] = _dot(wqt_ref[...], h2t)
    sk0 = sk_ref[0]
    sk1 = sk_ref[1]
    tt = h2t.shape[1]
    sub = lax.broadcasted_iota(jnp.int32, (SUBLANES, tt), 0)

    for hd in range(PEER_HEADS):
        base = hd * 2 * d_half
        s0 = _dot(sk0, qt_ref[base:base + d_half, :].astype(BF16))
        s1 = _dot(sk1, qt_ref[base + d_half:base + 2 * d_half, :].astype(BF16))
        rows0 = [s0[r * SUBLANES:(r + 1) * SUBLANES] for r in range(PEER_TOPK)]
        rows1 = [s1[r * SUBLANES:(r + 1) * SUBLANES] for r in range(PEER_TOPK)]
        a0 = _top16_sorted(rows0)
        a1 = _top16_sorted(rows1)
        a0_col = a0[SUBLANES - 1]
        for r0 in range(SUBLANES - 2, -1, -1):
            a0_col = jnp.where(sub == r0, a0[r0], a0_col)
        cand = [a0_col + a1[r1] for r1 in range(PEER_TOPK)]
        top = _merge_sublanes(cand)
        tail = [a0[SUBLANES + k] + a1[0] for k in range(SUBLANES)]
        top = _bitonic_to_sorted(
            top[:SUBLANES] + [jnp.maximum(top[SUBLANES + k], tail[SUBLANES - 1 - k])
                              for k in range(SUBLANES)])
        tau = top[PEER_TOPK - 1]
        z = jnp.ones_like(tau)
        for k in range(1, PEER_TOPK):
            z = z + jnp.exp(top[k] - top[0])
        cnt = jnp.zeros_like(tau)
        for r1 in range(PEER_TOPK):
            cnt = jnp.where(cand[r1] >= tau, float(r1 + 1), cnt)
        n_by_rank = [jnp.broadcast_to(cnt[q:q + 1], cnt.shape) for q in range(SUBLANES)]
        n0_rows = []
        rank1_rows = []
        for r in range(PEER_TOPK):
            n0 = jnp.where(rows0[r] >= a0[PEER_TOPK - 1],
                           jnp.where(rows0[r] + a1[0] >= tau, 1.0, 0.0), 0.0)
            rank1 = jnp.full(tau.shape, NO_RANK, F32)
            for q in range(PEER_TOPK - 1, -1, -1):
                if q < SUBLANES:
                    n0 = jnp.where(rows0[r] == a0[q], n_by_rank[q], n0)
                rank1 = jnp.where(rows1[r] >= a1[q], float(q), rank1)
            n0_rows.append(n0)
            rank1_rows.append(rank1)
        n0_ref[hd] = jnp.concatenate(n0_rows, axis=0)
        p0_ref[hd] = jnp.exp(s0 - jnp.concatenate([a0[0]] * PEER_TOPK, axis=0))
        rank1_ref[hd] = jnp.concatenate(rank1_rows, axis=0).astype(BF16)
        p1_ref[hd] = (jnp.exp(s1 - jnp.concatenate([a1[0]] * PEER_TOPK, axis=0))
                      / jnp.concatenate([z] * PEER_TOPK, axis=0)).astype(BF16)


def _peer_sel(x1, g2, wqt_bf, sk_bf, *, tt):
    t, d = x1.shape
    _, n_keys, d_half = sk_bf.shape
    dq = wqt_bf.shape[0]
    tab = lambda dt: jax.ShapeDtypeStruct((PEER_HEADS, n_keys, t), dt)
    tab_spec = pl.BlockSpec((PEER_HEADS, n_keys, tt), lambda i: (0, 0, i))
    return pl.pallas_call(
        functools.partial(_peer_sel_kernel, n_keys=n_keys, d_half=d_half),
        grid=(t // tt,),
        in_specs=[pl.BlockSpec((tt, d), lambda i: (i, 0)),
                  pl.BlockSpec((1, d), lambda i: (0, 0)),
                  pl.BlockSpec((dq, d), lambda i: (0, 0)),
                  pl.BlockSpec((2, n_keys, d_half), lambda i: (0, 0, 0))],
        out_specs=[pl.BlockSpec((d, tt), lambda i: (0, i)), tab_spec, tab_spec, tab_spec, tab_spec],
        out_shape=[jax.ShapeDtypeStruct((d, t), BF16), tab(BF16), tab(BF16), tab(F32), tab(F32)],
        scratch_shapes=[pltpu.VMEM((dq, tt), F32)],
        compiler_params=pltpu.CompilerParams(
            dimension_semantics=("arbitrary",), vmem_limit_bytes=VMEM_LIMIT_BYTES),
        name="peer_sel",
    )(x1, g2.reshape(1, d), wqt_bf, sk_bf)


def _peer_main_kernel(h2t_ref, rank1_ref, p1_ref, n0_ref, p0_ref, u_ref, vt_ref, x1_ref, gf_ref,
                      y_ref, acc_ref, a_ref, g_ref, *, n_keys, te, n_eblk, final_norm):
    s = pl.program_id(0)
    n_pairs = pl.num_programs(0) - 2
    tt = h2t_ref.shape[1]
    slot0 = s % 2
    slot1 = (s + 1) % 2

    @pl.when(s == 0)
    def _():
        a_ref[...] = jnp.zeros_like(a_ref)
        g_ref[...] = jnp.zeros_like(g_ref)

    pair3 = jnp.clip(s - 2, 0, n_pairs - 1)
    e3 = pair3 % n_eblk

    @pl.when(e3 == 0)
    def _():
        acc_ref[...] = jnp.zeros_like(acc_ref)

    acc_ref[...] += _dot(pltpu.bitcast(vt_ref[...], BF16), g_ref[slot0])

    e2 = jnp.clip(s - 1, 0, n_pairs - 1) % n_eblk
    for ii in range(te // n_keys):
        i = e2 * (te // n_keys) + ii
        rows = slice(ii * n_keys, (ii + 1) * n_keys)
        w = jnp.zeros((n_keys, tt), BF16)
        for hd in range(PEER_HEADS):
            n_row = jnp.broadcast_to(n0_ref[hd, pl.ds(i, 1), :], (16, tt)).astype(BF16)
            p_row = jnp.broadcast_to(p0_ref[hd, pl.ds(i, 1), :], (16, tt)).astype(BF16)
            n_b = jnp.broadcast_to(n_row[None], (n_keys // 16, 16, tt)).reshape(n_keys, tt)
            p_b = jnp.broadcast_to(p_row[None], (n_keys // 16, 16, tt)).reshape(n_keys, tt)
            w = w + jnp.where(rank1_ref[hd] < n_b, p1_ref[hd], jnp.zeros((), BF16)) * p_b
        a = a_ref[slot1, rows, :]
        act = (0.5 * a) * (1.0 + lax.erf(a * np.float32(2.0 ** -0.5).astype(BF16)))
        g_ref[slot1, rows, :] = w * act

    a_ref[slot0] = _dot(pltpu.bitcast(u_ref[...], BF16), h2t_ref[...]).astype(BF16)

    @pl.when(jnp.logical_and(e3 == n_eblk - 1, s >= 2))
    def _():
        out = x1_ref[...] + acc_ref[...].T
        if final_norm:
            out = _rmsnorm(out, gf_ref[...])
        y_ref[...] = out


def _peer_main(x1, h2t, rank1, p1, n0, p0, u_pk, vt_pk, gf, *, tt, te, final_norm):
    t, d = x1.shape
    n_exp = vt_pk.shape[1]
    n_keys = rank1.shape[1]
    n_eblk = n_exp // te
    n_pairs = (t // tt) * n_eblk

    def pair(s, lag):
        p = jnp.clip(s - lag, 0, n_pairs - 1)
        return p // n_eblk, p % n_eblk

    tab_spec = pl.BlockSpec((PEER_HEADS, n_keys, tt), lambda s: (0, 0, pair(s, 1)[0]))
    return pl.pallas_call(
        functools.partial(_peer_main_kernel, n_keys=n_keys, te=te, n_eblk=n_eblk,
                          final_norm=final_norm),
        grid=(n_pairs + 2,),
        in_specs=[pl.BlockSpec((d, tt), lambda s: (0, pair(s, 0)[0])),
                  tab_spec, tab_spec, tab_spec, tab_spec,
                  pl.BlockSpec((te // 2, d), lambda s: (pair(s, 0)[1], 0)),
                  pl.BlockSpec((d // 2, te), lambda s: (0, pair(s, 2)[1])),
                  pl.BlockSpec((tt, d), lambda s: (pair(s, 2)[0], 0)),
                  pl.BlockSpec((1, d), lambda s: (0, 0))],
        out_specs=pl.BlockSpec((tt, d), lambda s: (pair(s, 2)[0], 0)),
        out_shape=jax.ShapeDtypeStruct((t, d), F32),
        scratch_shapes=[pltpu.VMEM((d, tt), F32), pltpu.VMEM((2, te, tt), BF16),
                        pltpu.VMEM((2, te, tt), BF16)],
        compiler_params=pltpu.CompilerParams(
            dimension_semantics=("arbitrary",), vmem_limit_bytes=VMEM_LIMIT_BYTES),
        name="peer_main",
    )(h2t, rank1, p1, n0, p0, u_pk, vt_pk, x1, gf.reshape(1, d))


def _pack_kernel(x_ref, o_ref, *, transpose):
    x = x_ref[...]
    if transpose:
        x = x.T
    o_ref[...] = pltpu.bitcast(x.astype(BF16), jnp.uint32)


def _pack_bf16_pairs(x, *, transpose, rows):
    n, d = x.shape
    if transpose:
        out_shape, out_spec = (d // 2, n), pl.BlockSpec((d // 2, rows), lambda i: (0, i))
    else:
        out_shape, out_spec = (n // 2, d), pl.BlockSpec((rows // 2, d), lambda i: (i, 0))
    return pl.pallas_call(
        functools.partial(_pack_kernel, transpose=transpose),
        grid=(n // rows,),
        in_specs=[pl.BlockSpec((rows, d), lambda i: (i, 0))],
        out_specs=out_spec,
        out_shape=jax.ShapeDtypeStruct(out_shape, jnp.uint32),
        compiler_params=pltpu.CompilerParams(dimension_semantics=("arbitrary",)),
        name="pack_bf16_pairs",
    )(x)


def _pick_tile(n, cap):
    tile = min(n, cap)
    assert n % tile == 0, (n, tile)
    return tile


def kernel(x_prompt, x_sample, state_conv, state_ret, norm1_g, w_in, conv_w, w_out, norm2_g,
           peer_wq, peer_subkeys, peer_u, peer_v, final_g):
    depth = w_in.shape[0]
    d = x_prompt.shape[-1]
    assert peer_wq.shape[-1] == PEER_HEADS * 2 * peer_subkeys.shape[-1]
    assert peer_u.shape[1] == peer_subkeys.shape[2] ** 2
    streams = [(x_prompt, None, None, 0), (x_sample, state_conv, state_ret, PAST_LEN)]
    xs = [s[0] for s in streams]
    conv_out = [[] for _ in streams]
    ret_out = [[] for _ in streams]
    for layer in range(depth):
        win_bf = w_in[layer].astype(BF16)
        wout_bf = w_out[layer].astype(BF16)
        wqt_bf = peer_wq[layer].T.astype(BF16)
        sk_bf = peer_subkeys[layer].astype(BF16)
        u_pk = _pack_bf16_pairs(peer_u[layer], transpose=False, rows=1024)
        vt_pk = _pack_bf16_pairs(peer_v[layer], transpose=True, rows=512)
        for si, (_, cst, rst, pos0) in enumerate(streams):
            x = xs[si]
            b, l, _ = x.shape
            x1, nconv, nret = _mixer(
                x, None if cst is None else cst[layer], None if rst is None else rst[layer], pos0,
                norm1_g[layer], win_bf, conv_w[layer], wout_bf, tl=_pick_tile(l, 256))
            conv_out[si].append(nconv)
            ret_out[si].append(nret)
            x1 = x1.reshape(b * l, d)
            h2t, rank1, p1, n0, p0 = _peer_sel(x1, norm2_g[layer], wqt_bf, sk_bf,
                                               tt=_pick_tile(b * l, 256))
            y = _peer_main(x1, h2t, rank1, p1, n0, p0, u_pk, vt_pk, final_g,
                           tt=_pick_tile(b * l, 512), te=2048, final_norm=layer == depth - 1)
            xs[si] = y.reshape(b, l, d)
    return (xs[0], xs[1], jnp.stack(conv_out[0]), jnp.stack(ret_out[0]),
            jnp.stack(conv_out[1]), jnp.stack(ret_out[1]))
```

```python
import functools

import numpy as np
import jax
import jax.numpy as jnp
from jax import lax
from jax.experimental import pallas as pl
from jax.experimental.pallas import tpu as pltpu

F32 = jnp.float32
BF16 = jnp.bfloat16

EPS = 1e-6
ROPE_BASE = 10000.0
PAST_LEN = 1024
RET_HEADS = 4
PEER_HEADS = 8
PEER_TOPK = 16
LOG_DECAY = tuple(float(np.log1p(-np.exp2(np.float32(-5.0 - h)))) for h in range(RET_HEADS))
NEG_INF = float("-inf")
NO_RANK = 127.0

VMEM_LIMIT_BYTES = 56 * 1024 * 1024
LANES = 128


def _rmsnorm(x, g):
    return x * lax.rsqrt(jnp.mean(x * x, axis=-1, keepdims=True) + EPS) * g


def _dot(a, b):
    return jnp.dot(a, b, preferred_element_type=F32)


def _mixer_kernel(*refs, tl, d, has_state):
    if has_state:
        (x_ref, cos_ref, sin_ref, cst_ref, rst_ref, g1_ref, win_ref, cw_ref, wout_ref,
         x1_ref, nconv_ref, nret_ref) = refs
    else:
        (x_ref, cos_ref, sin_ref, g1_ref, win_ref, cw_ref, wout_ref,
         x1_ref, nconv_ref, nret_ref) = refs
    dk = d // RET_HEADS
    half = dk // 2

    @pl.when(pl.program_id(1) == 0)
    def _():
        if has_state:
            nconv_ref[...] = cst_ref[...]
            nret_ref[...] = rst_ref[...]
        else:
            nconv_ref[...] = jnp.zeros_like(nconv_ref)
            nret_ref[...] = jnp.zeros_like(nret_ref)

    x = x_ref[0]
    h = _rmsnorm(x, g1_ref[...]).astype(BF16)

    def proj(g):
        return _dot(h, win_ref[:, g * d:(g + 1) * d])

    u = proj(1) * proj(2)
    carry = nconv_ref[0]
    row = lax.broadcasted_iota(jnp.int32, (tl, d), 0)
    u1 = jnp.where(row == 0, carry[1:2], pltpu.roll(u, 1, 0))
    u2 = jnp.where(row == 0, carry[0:1], jnp.where(row == 1, carry[1:2], pltpu.roll(u, 2, 0)))
    cw = cw_ref[...]
    y_conv = proj(0) * (u2 * cw[0:1] + u1 * cw[1:2] + u * cw[2:3])
    nconv_ref[0] = u[tl - 2:tl, :]
    gate_a = jax.nn.sigmoid(proj(7))
    m_conv = gate_a * y_conv

    cos = cos_ref[...]
    sin = sin_ref[...]
    q = proj(3)
    k = proj(4)
    v = proj(5)
    n_idx = lax.broadcasted_iota(jnp.int32, (tl, tl), 0)
    m_idx = lax.broadcasted_iota(jnp.int32, (tl, tl), 1)
    diff = (n_idx - m_idx).astype(F32)
    rowf = lax.broadcasted_iota(jnp.int32, (tl, dk), 0).astype(F32)

    def rope(t):
        t1 = t[:, :half]
        t2 = t[:, half:]
        return jnp.concatenate([t1 * cos - t2 * sin, t1 * sin + t2 * cos], axis=-1)

    outs = []
    for hh in range(RET_HEADS):
        lg = LOG_DECAY[hh]
        sl = slice(hh * dk, (hh + 1) * dk)
        qr = rope(q[:, sl])
        kr = rope(k[:, sl]) * (dk ** -0.5)
        vb = v[:, sl].astype(BF16)
        qb = qr.astype(BF16)
        decay = jnp.where(diff >= 0, jnp.exp(lg * jnp.maximum(diff, 0.0)), 0.0)
        att = lax.dot_general(qb, kr.astype(BF16), (((1,), (1,)), ((), ())),
                              preferred_element_type=F32) * decay
        s_old = nret_ref[0, hh]
        o = _dot(att.astype(BF16), vb) + _dot(qb, s_old.astype(BF16)) * jnp.exp(lg * (rowf + 1.0))
        kw = (kr * jnp.exp(lg * (tl - 1.0 - rowf))).astype(BF16)
        nret_ref[0, hh] = float(np.exp(lg * tl)) * s_old + lax.dot_general(
            kw, vb, (((0,), (0,)), ((), ())), preferred_element_type=F32)
        outs.append(o * lax.rsqrt(jnp.mean(o * o, axis=-1, keepdims=True) + EPS))
    o_all = jnp.concatenate(outs, axis=-1)
    rg = proj(6)
    y_ret = rg * jax.nn.sigmoid(rg) * o_all
    m = m_conv + jax.nn.sigmoid(proj(8)) * y_ret
    x1_ref[0] = x + _dot(m.astype(BF16), wout_ref[...])


def _mixer(x, conv_state, ret_state, pos0, g1, win_bf, cw, wout_bf, *, tl):
    b, l, d = x.shape
    dk = d // RET_HEADS
    half = dk // 2
    has_state = conv_state is not None
    inv = ROPE_BASE ** (-jnp.arange(half, dtype=F32) / half)
    ang = (pos0 + jnp.arange(l)).astype(F32)[:, None] * inv[None, :]
    cos, sin = jnp.cos(ang), jnp.sin(ang)

    const = lambda *shape: pl.BlockSpec(shape, lambda i, j: (0,) * len(shape))
    in_specs = [pl.BlockSpec((1, tl, d), lambda i, j: (i, j, 0)),
                pl.BlockSpec((tl, half), lambda i, j: (j, 0)),
                pl.BlockSpec((tl, half), lambda i, j: (j, 0))]
    args = [x, cos, sin]
    if has_state:
        in_specs += [pl.BlockSpec((1, 2, d), lambda i, j: (i, 0, 0)),
                     pl.BlockSpec((1, RET_HEADS, dk, dk), lambda i, j: (i, 0, 0, 0))]
        args += [conv_state, ret_state]
    in_specs += [const(1, d), const(d, win_bf.shape[1]), const(3, d), const(d, d)]
    args += [g1.reshape(1, d), win_bf, cw, wout_bf]
    return pl.pallas_call(
        functools.partial(_mixer_kernel, tl=tl, d=d, has_state=has_state),
        grid=(b, l // tl),
        in_specs=in_specs,
        out_specs=[pl.BlockSpec((1, tl, d), lambda i, j: (i, j, 0)),
                   pl.BlockSpec((1, 2, d), lambda i, j: (i, 0, 0)),
                   pl.BlockSpec((1, RET_HEADS, dk, dk), lambda i, j: (i, 0, 0, 0))],
        out_shape=[jax.ShapeDtypeStruct((b, l, d), F32),
                   jax.ShapeDtypeStruct((b, 2, d), F32),
                   jax.ShapeDtypeStruct((b, RET_HEADS, dk, dk), F32)],
        compiler_params=pltpu.CompilerParams(
            dimension_semantics=("arbitrary", "arbitrary"), vmem_limit_bytes=VMEM_LIMIT_BYTES),
        name="mixer",
    )(*args)


def _oddeven_merge_sort_pairs(n):
    pairs = []
    p = 1
    while p < n:
        k = p
        while k >= 1:
            for j in range(k % p, n - k, 2 * k):
                for i in range(min(k, n - j - k)):
                    if (i + j) // (2 * p) == (i + j + k) // (2 * p):
                        pairs.append((i + j, i + j + k))
            k //= 2
        p *= 2
    return pairs


SORT16_PAIRS = tuple(_oddeven_merge_sort_pairs(PEER_TOPK))
SUBLANES = 8


def _bitonic_to_sorted(z):
    z = list(z)
    d = PEER_TOPK // 2
    while d >= 1:
        for r in range(PEER_TOPK):
            if r % (2 * d) < d:
                hi = jnp.maximum(z[r], z[r + d])
                lo = jnp.minimum(z[r], z[r + d])
                z[r], z[r + d] = hi, lo
        d //= 2
    return z


def _merge_sublanes(x):
    shift = SUBLANES // 2
    while shift >= 1:
        other = [pltpu.roll(x[PEER_TOPK - 1 - r], shift, 0) for r in range(PEER_TOPK)]
        x = _bitonic_to_sorted([jnp.maximum(x[r], other[r]) for r in range(PEER_TOPK)])
        shift //= 2
    return x


def _top16_sorted(rows):
    x = list(rows)
    for a, b in SORT16_PAIRS:
        hi = jnp.maximum(x[a], x[b])
        lo = jnp.minimum(x[a], x[b])
        x[a], x[b] = hi, lo
    return _merge_sublanes(x)


def _peer_sel_kernel(x1_ref, g2_ref, wqt_ref, sk_ref,
                     h2t_ref, rank1_ref, p1_ref, n0_ref, p0_ref, qt_ref, *, n_keys, d_half):
    assert n_keys == PEER_TOPK * SUBLANES
    h2 = _rmsnorm(x1_ref[...], g2_ref[...])
    h2t = h2.T.astype(BF16)
    h2t_ref[...] = h2t
    qt_ref[...] = _dot(wqt_ref[...], h2t)
    sk0 = sk_ref[0]
    sk1 = sk_ref[1]
    tt = h2t.shape[1]
    sub = lax.broadcasted_iota(jnp.int32, (SUBLANES, tt), 0)

    for hd in range(PEER_HEADS):
        base = hd * 2 * d_half
        s0 = _dot(sk0, qt_ref[base:base + d_half, :].astype(BF16))
        s1 = _dot(sk1, qt_ref[base + d_half:base + 2 * d_half, :].astype(BF16))
        rows0 = [s0[r * SUBLANES:(r + 1) * SUBLANES] for r in range(PEER_TOPK)]
        rows1 = [s1[r * SUBLANES:(r + 1) * SUBLANES] for r in range(PEER_TOPK)]
        a0 = _top16_sorted(rows0)
        a1 = _top16_sorted(rows1)
        a0_col = a0[SUBLANES - 1]
        for r0 in range(SUBLANES - 2, -1, -1):
            a0_col = jnp.where(sub == r0, a0[r0], a0_col)
        cand = [a0_col + a1[r1] for r1 in range(PEER_TOPK)]
        top = _merge_sublanes(cand)
        tail = [a0[SUBLANES + k] + a1[0] for k in range(SUBLANES)]
        top = _bitonic_to_sorted(
            top[:SUBLANES] + [jnp.maximum(top[SUBLANES + k], tail[SUBLANES - 1 - k])
                              for k in range(SUBLANES)])
        tau = top[PEER_TOPK - 1]
        z = jnp.ones_like(tau)
        for k in range(1, PEER_TOPK):
            z = z + jnp.exp(top[k] - top[0])
        cnt = jnp.zeros_like(tau)
        for r1 in range(PEER_TOPK):
            cnt = jnp.where(cand[r1] >= tau, float(r1 + 1), cnt)
        n_by_rank = [jnp.broadcast_to(cnt[q:q + 1], cnt.shape) for q in range(SUBLANES)]
        n0_rows = []
        rank1_rows = []
        for r in range(PEER_TOPK):
            n0 = jnp.where(rows0[r] >= a0[PEER_TOPK - 1],
                           jnp.where(rows0[r] + a1[0] >= tau, 1.0, 0.0), 0.0)
            rank1 = jnp.full(tau.shape, NO_RANK, F32)
            for q in range(PEER_TOPK - 1, -1, -1):
                if q < SUBLANES:
                    n0 = jnp.where(rows0[r] == a0[q], n_by_rank[q], n0)
                rank1 = jnp.where(rows1[r] >= a1[q], float(q), rank1)
            n0_rows.append(n0)
            rank1_rows.append(rank1)
        n0_ref[hd] = jnp.concatenate(n0_rows, axis=0)
        p0_ref[hd] = jnp.exp(s0 - jnp.concatenate([a0[0]] * PEER_TOPK, axis=0))
        rank1_ref[hd] = jnp.concatenate(rank1_rows, axis=0).astype(BF16)
        p1_ref[hd] = (jnp.exp(s1 - jnp.concatenate([a1[0]] * PEER_TOPK, axis=0))
                      / jnp.concatenate([z] * PEER_TOPK, axis=0)).astype(BF16)


def _peer_sel(x1, g2, wqt_bf, sk_bf, *, tt):
    t, d = x1.shape
    _, n_keys, d_half = sk_bf.shape
    dq = wqt_bf.shape[0]
    tab = lambda dt: jax.ShapeDtypeStruct((PEER_HEADS, n_keys, t), dt)
    tab_spec = pl.BlockSpec((PEER_HEADS, n_keys, tt), lambda i: (0, 0, i))
    return pl.pallas_call(
        functools.partial(_peer_sel_kernel, n_keys=n_keys, d_half=d_half),
        grid=(t // tt,),
        in_specs=[pl.BlockSpec((tt, d), lambda i: (i, 0)),
                  pl.BlockSpec((1, d), lambda i: (0, 0)),
                  pl.BlockSpec((dq, d), lambda i: (0, 0)),
                  pl.BlockSpec((2, n_keys, d_half), lambda i: (0, 0, 0))],
        out_specs=[pl.BlockSpec((d, tt), lambda i: (0, i)), tab_spec, tab_spec, tab_spec, tab_spec],
        out_shape=[jax.ShapeDtypeStruct((d, t), BF16), tab(BF16), tab(BF16), tab(F32), tab(F32)],
        scratch_shapes=[pltpu.VMEM((dq, tt), F32)],
        compiler_params=pltpu.CompilerParams(
            dimension_semantics=("arbitrary",), vmem_limit_bytes=VMEM_LIMIT_BYTES),
        name="peer_sel",
    )(x1, g2.reshape(1, d), wqt_bf, sk_bf)


def _peer_main_kernel(h2t_ref, rank1_ref, p1_ref, n0_ref, p0_ref, u_ref, vt_ref, x1_ref, gf_ref,
                      y_ref, acc_ref, a_ref, *, n_keys, te, n_eblk, final_norm):
    s = pl.program_id(0)
    n_pairs = pl.num_programs(0) - 1
    tt = h2t_ref.shape[1]
    slot0 = s % 2
    slot1 = (s + 1) % 2

    @pl.when(s == 0)
    def _():
        a_ref[...] = jnp.zeros_like(a_ref)

    e2 = jnp.clip(s - 1, 0, n_pairs - 1) % n_eblk
    e3 = e2

    @pl.when(e3 == 0)
    def _():
        acc_ref[...] = jnp.zeros_like(acc_ref)

    gated = []
    for ii in range(te // n_keys):
        i = e2 * (te // n_keys) + ii
        rows = slice(ii * n_keys, (ii + 1) * n_keys)
        w = jnp.zeros((n_keys, tt), BF16)
        for hd in range(PEER_HEADS):
            n_row = jnp.broadcast_to(n0_ref[hd, pl.ds(i, 1), :], (16, tt)).astype(BF16)
            p_row = jnp.broadcast_to(p0_ref[hd, pl.ds(i, 1), :], (16, tt)).astype(BF16)
            n_b = jnp.broadcast_to(n_row[None], (n_keys // 16, 16, tt)).reshape(n_keys, tt)
            p_b = jnp.broadcast_to(p_row[None], (n_keys // 16, 16, tt)).reshape(n_keys, tt)
            w = w + jnp.where(rank1_ref[hd] < n_b, p1_ref[hd], jnp.zeros((), BF16)) * p_b
        a = a_ref[slot1, rows, :]
        act = (0.5 * a) * (1.0 + lax.erf(a * np.float32(2.0 ** -0.5).astype(BF16)))
        gated.append(w * act)
    acc_ref[...] += _dot(pltpu.bitcast(vt_ref[...], BF16), jnp.concatenate(gated, axis=0))

    a_ref[slot0] = _dot(pltpu.bitcast(u_ref[...], BF16), h2t_ref[...]).astype(BF16)

    @pl.when(jnp.logical_and(e3 == n_eblk - 1, s >= 1))
    def _():
        out = x1_ref[...] + acc_ref[...].T
        if final_norm:
            out = _rmsnorm(out, gf_ref[...])
        y_ref[...] = out


def _peer_main(x1, h2t, rank1, p1, n0, p0, u_pk, vt_pk, gf, *, tt, te, final_norm):
    t, d = x1.shape
    n_exp = vt_pk.shape[1]
    n_keys = rank1.shape[1]
    n_eblk = n_exp // te
    n_pairs = (t // tt) * n_eblk

    def pair(s, lag):
        p = jnp.clip(s - lag, 0, n_pairs - 1)
        return p // n_eblk, p % n_eblk

    tab_spec = pl.BlockSpec((PEER_HEADS, n_keys, tt), lambda s: (0, 0, pair(s, 1)[0]))
    return pl.pallas_call(
        functools.partial(_peer_main_kernel, n_keys=n_keys, te=te, n_eblk=n_eblk,
                          final_norm=final_norm),
        grid=(n_pairs + 1,),
        in_specs=[pl.BlockSpec((d, tt), lambda s: (0, pair(s, 0)[0])),
                  tab_spec, tab_spec, tab_spec, tab_spec,
                  pl.BlockSpec((te // 2, d), lambda s: (pair(s, 0)[1], 0)),
                  pl.BlockSpec((d // 2, te), lambda s: (0, pair(s, 1)[1])),
                  pl.BlockSpec((tt, d), lambda s: (pair(s, 1)[0], 0)),
                  pl.BlockSpec((1, d), lambda s: (0, 0))],
        out_specs=pl.BlockSpec((tt, d), lambda s: (pair(s, 1)[0], 0)),
        out_shape=jax.ShapeDtypeStruct((t, d), F32),
        scratch_shapes=[pltpu.VMEM((d, tt), F32), pltpu.VMEM((2, te, tt), BF16)],
        compiler_params=pltpu.CompilerParams(
            dimension_semantics=("arbitrary",), vmem_limit_bytes=VMEM_LIMIT_BYTES),
        name="peer_main",
    )(h2t, rank1, p1, n0, p0, u_pk, vt_pk, x1, gf.reshape(1, d))


def _pack_kernel(x_ref, o_ref, *, transpose):
    x = x_ref[...]
    if transpose:
        x = x.T
    o_ref[...] = pltpu.bitcast(x.astype(BF16), jnp.uint32)


def _pack_bf16_pairs(x, *, transpose, rows):
    n, d = x.shape
    if transpose:
        out_shape, out_spec = (d // 2, n), pl.BlockSpec((d // 2, rows), lambda i: (0, i))
    else:
        out_shape, out_spec = (n // 2, d), pl.BlockSpec((rows // 2, d), lambda i: (i, 0))
    return pl.pallas_call(
        functools.partial(_pack_kernel, transpose=transpose),
        grid=(n // rows,),
        in_specs=[pl.BlockSpec((rows, d), lambda i: (i, 0))],
        out_specs=out_spec,
        out_shape=jax.ShapeDtypeStruct(out_shape, jnp.uint32),
        compiler_params=pltpu.CompilerParams(dimension_semantics=("arbitrary",)),
        name="pack_bf16_pairs",
    )(x)


def _pick_tile(n, cap):
    tile = min(n, cap)
    assert n % tile == 0, (n, tile)
    return tile


def kernel(x_prompt, x_sample, state_conv, state_ret, norm1_g, w_in, conv_w, w_out, norm2_g,
           peer_wq, peer_subkeys, peer_u, peer_v, final_g):
    depth = w_in.shape[0]
    d = x_prompt.shape[-1]
    assert peer_wq.shape[-1] == PEER_HEADS * 2 * peer_subkeys.shape[-1]
    assert peer_u.shape[1] == peer_subkeys.shape[2] ** 2
    streams = [(x_prompt, None, None, 0), (x_sample, state_conv, state_ret, PAST_LEN)]
    xs = [s[0] for s in streams]
    conv_out = [[] for _ in streams]
    ret_out = [[] for _ in streams]
    for layer in range(depth):
        win_bf = w_in[layer].astype(BF16)
        wout_bf = w_out[layer].astype(BF16)
        wqt_bf = peer_wq[layer].T.astype(BF16)
        sk_bf = peer_subkeys[layer].astype(BF16)
        u_pk = _pack_bf16_pairs(peer_u[layer], transpose=False, rows=1024)
        vt_pk = _pack_bf16_pairs(peer_v[layer], transpose=True, rows=512)
        for si, (_, cst, rst, pos0) in enumerate(streams):
            x = xs[si]
            b, l, _ = x.shape
            x1, nconv, nret = _mixer(
                x, None if cst is None else cst[layer], None if rst is None else rst[layer], pos0,
                norm1_g[layer], win_bf, conv_w[layer], wout_bf, tl=_pick_tile(l, 256))
            conv_out[si].append(nconv)
            ret_out[si].append(nret)
            x1 = x1.reshape(b * l, d)
            h2t, rank1, p1, n0, p0 = _peer_sel(x1, norm2_g[layer], wqt_bf, sk_bf,
                                               tt=_pick_tile(b * l, 256))
            y = _peer_main(x1, h2t, rank1, p1, n0, p0, u_pk, vt_pk, final_g,
                           tt=_pick_tile(b * l, 512), te=2048, final_norm=layer == depth - 1)
            xs[si] = y.reshape(b, l, d)
    return (xs[0], xs[1], jnp.stack(conv_out[0]), jnp.stack(ret_out[0]),
            jnp.stack(conv_out[1]), jnp.stack(ret_out[1]))
```
